```python
import math
import jax
import jax.numpy as jnp
from jax import lax
import numpy as np

D_MODEL = 1024
BATCH = 1
SEQ = 16384
DEPTH = 1

ATTN_HEADS = 8
KV_HEADS = 2
HEAD_DIM = 64
Q_PER_KV = ATTN_HEADS // KV_HEADS
ATTN_WIDTH = ATTN_HEADS * HEAD_DIM
KV_WIDTH = KV_HEADS * HEAD_DIM
SSM_WIDTH = D_MODEL - ATTN_WIDTH
SSM_GROUP = 16
SSM_GROUPS = SSM_WIDTH // SSM_GROUP
SSM_STATE = 64
CMP_BLOCK = 32
CMP_STRIDE = 16
CMP_HIDDEN = 256
SEL_BLOCK = 64
SEL_TOPN = 16
WINDOW = 512
Q_BLOCK = 128
FORCE_SCORE = 1.0e4
ROPE_THETA = 10000.0
FFN_DIM = 2816
CONV_WIDTH = 3
NORM_EPS = 1e-6
IN_WIDTH = ATTN_WIDTH + 6 * KV_WIDTH + 3 * ATTN_HEADS + SSM_WIDTH

kernel_name = "hymba_nsa_s5_convffn_layer"


def rmsnorm(x, g):
    xf = x.astype(jnp.float32)
    y = xf * lax.rsqrt(jnp.mean(xf * xf, axis=-1, keepdims=True) + NORM_EPS)
    return (y * g.astype(jnp.float32)).astype(x.dtype)


def rope(x, pos):
    half = HEAD_DIM // 2
    inv = jnp.power(ROPE_THETA, -jnp.arange(half, dtype=jnp.float32) * 2.0 / HEAD_DIM)
    ang = pos.astype(jnp.float32)[:, None] * inv[None, :]
    cos = jnp.cos(ang)[None, :, None, :]
    sin = jnp.sin(ang)[None, :, None, :]
    xf = x.astype(jnp.float32)
    x1, x2 = xf[..., :half], xf[..., half:]
    return jnp.concatenate([x1 * cos - x2 * sin, x1 * sin + x2 * cos], axis=-1).astype(x.dtype)


def masked_softmax(s, mask):
    s = jnp.where(mask, s.astype(jnp.float32), -jnp.inf)
    m = jnp.max(s, axis=-1, keepdims=True)
    m = jnp.where(jnp.isfinite(m), m, 0.0)
    e = jnp.where(mask, jnp.exp(s - m), 0.0)
    return e / jnp.maximum(jnp.sum(e, axis=-1, keepdims=True), 1e-30)


def compress_blocks(kv, pe, w1, w2):
    b, l = kv.shape[0], kv.shape[1]
    n_cmp = (l - CMP_BLOCK) // CMP_STRIDE + 1
    idx = jnp.arange(n_cmp)[:, None] * CMP_STRIDE + jnp.arange(CMP_BLOCK)[None, :]
    blocks = kv[:, idx] + pe[None, None, :, None, :]
    flat = blocks.transpose(0, 1, 3, 2, 4).reshape(b, n_cmp, KV_HEADS, CMP_BLOCK * HEAD_DIM)
    out = jax.nn.gelu(flat @ w1) @ w2
    return out.transpose(0, 2, 1, 3)


def nsa_attention(q, k_c, v_c, k_s, v_s, k_w, v_w, gates, pe_k, pe_v, w_ck1, w_ck2, w_cv1, w_cv2):
    b, l = q.shape[0], q.shape[1]
    pos = jnp.arange(l)
    q = rope(q, pos)
    k_c = rope(k_c, pos)
    k_s = rope(k_s, pos)
    k_w = rope(k_w, pos)
    kc = compress_blocks(k_c, pe_k, w_ck1, w_ck2)
    vc = compress_blocks(v_c, pe_v, w_cv1, w_cv2)
    n_cmp = kc.shape[2]
    c_start = jnp.arange(n_cmp) * CMP_STRIDE
    c_end = c_start + CMP_BLOCK - 1
    n_sel = l // SEL_BLOCK
    top_n = min(SEL_TOPN, n_sel)
    s_start = jnp.arange(n_sel) * SEL_BLOCK
    overlap = jnp.clip(
        jnp.minimum(c_start[:, None] + CMP_BLOCK, s_start[None, :] + SEL_BLOCK)
        - jnp.maximum(c_start[:, None], s_start[None, :]), 0, None).astype(jnp.float32) / CMP_BLOCK
    ks_blk = k_s.reshape(b, n_sel, SEL_BLOCK, KV_HEADS, HEAD_DIM).transpose(0, 3, 1, 2, 4)
    vs_blk = v_s.reshape(b, n_sel, SEL_BLOCK, KV_HEADS, HEAD_DIM).transpose(0, 3, 1, 2, 4)
    pad = ((0, 0), (WINDOW, 0), (0, 0), (0, 0))
    kw_pad = jnp.pad(k_w, pad).transpose(0, 2, 1, 3)
    vw_pad = jnp.pad(v_w, pad).transpose(0, 2, 1, 3)
    n_blk = l // Q_BLOCK
    q_blk = q.reshape(b, n_blk, Q_BLOCK, KV_HEADS, Q_PER_KV, HEAD_DIM).transpose(1, 0, 3, 4, 2, 5)
    scale = HEAD_DIM ** -0.5
    b_idx = jnp.arange(b)[:, None, None, None]
    h_idx = jnp.arange(KV_HEADS)[None, :, None, None]
    sel_ids = jnp.arange(n_sel)
    win_off = jnp.arange(WINDOW + Q_BLOCK) - WINDOW
    sub = jnp.arange(SEL_BLOCK)

    def attend_block(args):
        qi, i = args
        t = i * Q_BLOCK + jnp.arange(Q_BLOCK)
        s_c = jnp.einsum('bhgqd,bhnd->bhgqn', qi, kc) * scale
        p_c = masked_softmax(s_c, c_end[None, :] <= t[:, None])
        o_c = jnp.einsum('bhgqn,bhnd->bhgqd', p_c.astype(vc.dtype), vc)
        imp = jnp.einsum('bhgqn,ns->bhqs', p_c, overlap)
        cur = (t // SEL_BLOCK)[:, None]
        forced = (sel_ids == 0) | (sel_ids == cur) | (sel_ids == cur - 1)
        score = jnp.where(sel_ids > cur, -jnp.inf, jnp.where(forced, FORCE_SCORE, imp))
        top_s, top_i = lax.top_k(score, top_n)
        k_g = ks_blk[b_idx, h_idx, top_i].reshape(b, KV_HEADS, Q_BLOCK, top_n * SEL_BLOCK, HEAD_DIM)
        v_g = vs_blk[b_idx, h_idx, top_i].reshape(b, KV_HEADS, Q_BLOCK, top_n * SEL_BLOCK, HEAD_DIM)
        key_pos = (top_i[..., None] * SEL_BLOCK + sub).reshape(b, KV_HEADS, Q_BLOCK, top_n * SEL_BLOCK)
        valid = jnp.repeat(jnp.isfinite(top_s), SEL_BLOCK, axis=-1) & (key_pos <= t[:, None])
        s_s = jnp.einsum('bhgqd,bhqkd->bhgqk', qi, k_g) * scale
        p_s = masked_softmax(s_s, valid[:, :, None])
        o_s = jnp.einsum('bhgqk,bhqkd->bhgqd', p_s.astype(v_g.dtype), v_g)
        k_wi = lax.dynamic_slice_in_dim(kw_pad, i * Q_BLOCK, WINDOW + Q_BLOCK, axis=2)
        v_wi = lax.dynamic_slice_in_dim(vw_pad, i * Q_BLOCK, WINDOW + Q_BLOCK, axis=2)
        w_pos = i * Q_BLOCK + win_off
        m_w = (w_pos[None, :] <= t[:, None]) & (w_pos[None, :] > t[:, None] - WINDOW) & (w_pos[None, :] >= 0)
        s_w = jnp.einsum('bhgqd,bhkd->bhgqk', qi, k_wi) * scale
        p_w = masked_softmax(s_w, m_w)
        o_w = jnp.einsum('bhgqk,bhkd->bhgqd', p_w.astype(v_wi.dtype), v_wi)
        return o_c, o_s, o_w

    o_c, o_s, o_w = lax.map(attend_block, (q_blk, jnp.arange(n_blk)))

    def unblock(o):
        return o.transpose(1, 0, 4, 2, 3, 5).reshape(b, l, ATTN_HEADS, HEAD_DIM)

    out = (gates[..., 0:1] * unblock(o_c) + gates[..., 1:2] * unblock(o_s)
           + gates[..., 2:3] * unblock(o_w))
    return out.reshape(b, l, ATTN_WIDTH)


def s5_mixer(u, lam_re, lam_im, log_step, b_re, b_im, c_re, c_im, d_skip, w_glu, b_glu):
    b, l, _ = u.shape
    uf = u.astype(jnp.float32).reshape(b, l, SSM_GROUPS, SSM_GROUP)
    step = jnp.exp(log_step.astype(jnp.float32))[:, None]
    lr = lam_re.astype(jnp.float32)
    li = lam_im.astype(jnp.float32)
    mag = jnp.exp(lr * step)
    a_re = mag * jnp.cos(li * step)
    a_im = mag * jnp.sin(li * step)
    den = lr * lr + li * li
    n_re = a_re - 1.0
    f_re = (n_re * lr + a_im * li) / den
    f_im = (a_im * lr - n_re * li) / den
    br = b_re.astype(jnp.float32)
    bi = b_im.astype(jnp.float32)
    bb_re = f_re[..., None] * br - f_im[..., None] * bi
    bb_im = f_re[..., None] * bi + f_im[..., None] * br
    bu_re = jnp.einsum('blgc,gpc->blgp', uf, bb_re)
    bu_im = jnp.einsum('blgc,gpc->blgp', uf, bb_im)
    a_re_t = jnp.broadcast_to(a_re, bu_re.shape)
    a_im_t = jnp.broadcast_to(a_im, bu_re.shape)

    def combine(e1, e2):
        a1r, a1i, b1r, b1i = e1
        a2r, a2i, b2r, b2i = e2
        return (a2r * a1r - a2i * a1i, a2r * a1i + a2i * a1r,
                a2r * b1r - a2i * b1i + b2r, a2r * b1i + a2i * b1r + b2i)

    _, _, x_re, x_im = lax.associative_scan(combine, (a_re_t, a_im_t, bu_re, bu_im), axis=1)
    y = (jnp.einsum('blgp,gcp->blgc', x_re, c_re.astype(jnp.float32))
         - jnp.einsum('blgp,gcp->blgc', x_im, c_im.astype(jnp.float32))
         + d_skip.astype(jnp.float32) * uf)
    y = jax.nn.gelu(y.reshape(b, l, SSM_WIDTH))
    y = y * jax.nn.sigmoid(y @ w_glu.astype(jnp.float32) + b_glu.astype(jnp.float32))
    return y.astype(u.dtype)


def causal_dwconv(u, w, bias):
    ch = u.shape[-1]
    y = lax.conv_general_dilated(u, w[:, None, :].astype(u.dtype), window_strides=(1,),
                                 padding=[(CONV_WIDTH - 1, 0)],
                                 dimension_numbers=('NWC', 'WIO', 'NWC'),
                                 feature_group_count=ch)
    return y + bias


def split_heads(t, n_heads):
    return t.reshape(t.shape[0], t.shape[1], n_heads, HEAD_DIM)


def setup_inputs(seed: int = 0) -> dict:
    key = jax.random.key(seed)
    ks = jax.random.split(key, 32)
    f32 = jnp.float32

    def nrm(k, shape, s):
        return jax.random.normal(k, shape, f32) * s

    D, F, NL = D_MODEL, FFN_DIM, DEPTH
    G, P, C = SSM_GROUPS, SSM_STATE, SSM_GROUP
    return {
        "x": nrm(ks[0], (BATCH, SEQ, D), 1.0),
        "c": nrm(ks[1], (BATCH, D), 1.0),
        "norm1_g": 1.0 + nrm(ks[2], (NL, D), 0.02),
        "norm2_g": 1.0 + nrm(ks[3], (NL, D), 0.02),
        "normf_g": 1.0 + nrm(ks[4], (D,), 0.02),
        "w_ada": nrm(ks[5], (NL, D, 6 * D), D ** -0.5),
        "b_ada": nrm(ks[6], (NL, 6 * D), 0.01),
        "w_in": nrm(ks[7], (NL, D, IN_WIDTH), D ** -0.5),
        "pe_k": nrm(ks[8], (NL, CMP_BLOCK, HEAD_DIM), 0.02),
        "pe_v": nrm(ks[9], (NL, CMP_BLOCK, HEAD_DIM), 0.02),
        "w_ck1": nrm(ks[10], (NL, CMP_BLOCK * HEAD_DIM, CMP_HIDDEN), (CMP_BLOCK * HEAD_DIM) ** -0.5),
        "w_ck2": nrm(ks[11], (NL, CMP_HIDDEN, HEAD_DIM), CMP_HIDDEN ** -0.5),
        "w_cv1": nrm(ks[12], (NL, CMP_BLOCK * HEAD_DIM, CMP_HIDDEN), (CMP_BLOCK * HEAD_DIM) ** -0.5),
        "w_cv2": nrm(ks[13], (NL, CMP_HIDDEN, HEAD_DIM), CMP_HIDDEN ** -0.5),
        "lam_re": -0.5 + nrm(ks[14], (NL, G, P), 0.01),
        "lam_im": math.pi * jnp.arange(P, dtype=f32) + nrm(ks[15], (NL, G, P), 0.01),
        "log_step": jax.random.uniform(ks[16], (NL, G), f32, math.log(1e-3), math.log(1e-1)),
        "b_re": nrm(ks[17], (NL, G, P, C), (2 * C) ** -0.5),
        "b_im": nrm(ks[18], (NL, G, P, C), (2 * C) ** -0.5),
        "c_re": nrm(ks[19], (NL, G, C, P), P ** -0.5),
        "c_im": nrm(ks[20], (NL, G, C, P), P ** -0.5),
        "d_skip": nrm(ks[21], (NL, G, C), 1.0),
        "w_glu": nrm(ks[22], (NL, SSM_WIDTH, SSM_WIDTH), SSM_WIDTH ** -0.5),
        "b_glu": nrm(ks[23], (NL, SSM_WIDTH), 0.01),
        "attn_norm_g": 1.0 + nrm(ks[24], (NL, ATTN_WIDTH), 0.02),
        "ssm_norm_g": 1.0 + nrm(ks[25], (NL, SSM_WIDTH), 0.02),
        "w_o": nrm(ks[26], (NL, D, D), D ** -0.5),
        "w_up": nrm(ks[27], (NL, D, 2 * F), D ** -0.5),
        "conv_w": nrm(ks[28], (NL, CONV_WIDTH, 2 * F), CONV_WIDTH ** -0.5),
        "conv_b": nrm(ks[29], (NL, 2 * F), 0.01),
        "w_down": nrm(ks[30], (NL, F, D), F ** -0.5),
    }


def reference(x, c, norm1_g, norm2_g, normf_g, w_ada, b_ada, w_in, pe_k, pe_v, w_ck1, w_ck2,
              w_cv1, w_cv2, lam_re, lam_im, log_step, b_re, b_im, c_re, c_im, d_skip, w_glu,
              b_glu, attn_norm_g, ssm_norm_g, w_o, w_up, conv_w, conv_b, w_down):
    b, l, _ = x.shape
    sizes = [ATTN_WIDTH] + [KV_WIDTH] * 6 + [3 * ATTN_HEADS]
    cuts = [int(v) for v in np.cumsum(sizes)]
    for li in range(DEPTH):
        mod = (jax.nn.silu(c) @ w_ada[li] + b_ada[li])[:, None, :]
        sh1, sc1, gt1, sh2, sc2, gt2 = jnp.split(mod, 6, axis=-1)
        h = rmsnorm(x, norm1_g[li]) * (1.0 + sc1) + sh1
        z = h @ w_in[li]
        zq, zkc, zvc, zks, zvs, zkw, zvw, zg, zs = jnp.split(z, cuts, axis=-1)
        gates = jax.nn.sigmoid(zg.reshape(b, l, ATTN_HEADS, 3))
        o_attn = nsa_attention(split_heads(zq, ATTN_HEADS), split_heads(zkc, KV_HEADS),
                               split_heads(zvc, KV_HEADS), split_heads(zks, KV_HEADS),
                               split_heads(zvs, KV_HEADS), split_heads(zkw, KV_HEADS),
                               split_heads(zvw, KV_HEADS), gates, pe_k[li], pe_v[li],
                               w_ck1[li], w_ck2[li], w_cv1[li], w_cv2[li])
        y_ssm = s5_mixer(zs, lam_re[li], lam_im[li], log_step[li], b_re[li], b_im[li],
                         c_re[li], c_im[li], d_skip[li], w_glu[li], b_glu[li])
        mix = jnp.concatenate([rmsnorm(o_attn, attn_norm_g[li]), rmsnorm(y_ssm, ssm_norm_g[li])], axis=-1)
        x = x + gt1 * (mix @ w_o[li])
        h = rmsnorm(x, norm2_g[li]) * (1.0 + sc2) + sh2
        u = causal_dwconv(h @ w_up[li], conv_w[li], conv_b[li])
        a, v = jnp.split(u, 2, axis=-1)
        x = x + gt2 * ((jax.nn.silu(a) * v) @ w_down[li])
    return rmsnorm(x, normf_g)
```

```python
import functools
import math

import jax
import jax.numpy as jnp
import numpy as np
from jax.experimental import pallas as pl
from jax.experimental.pallas import tpu as pltpu

ATTN_HEADS = 8
KV_HEADS = 2
HEAD_DIM = 64
Q_PER_KV = ATTN_HEADS // KV_HEADS
ATTN_WIDTH = ATTN_HEADS * HEAD_DIM
KV_WIDTH = KV_HEADS * HEAD_DIM
SSM_GROUP = 16
SSM_STATE = 64
CMP_BLOCK = 32
CMP_STRIDE = 16
SEL_BLOCK = 64
SEL_TOPN = 16
WINDOW = 512
FORCE_SCORE = 1.0e4
ROPE_THETA = 10000.0
CONV_WIDTH = 3
NORM_EPS = 1e-6

LANES = 128
Q_TILE = 128
KEY_CHUNK = 512
ROW_TILE = 512
WIN_CHUNKS = WINDOW // Q_TILE + 1
VMEM_LIMIT = 56 * 1024 * 1024

F32 = jnp.float32
BF16 = jnp.bfloat16
NEG_INF = float("-inf")


def _params(n_axes, vmem=VMEM_LIMIT):
    return pltpu.CompilerParams(dimension_semantics=("arbitrary",) * n_axes,
                                vmem_limit_bytes=vmem)


def _split(a):
    hi = a.astype(BF16)
    lo = (a - hi.astype(F32)).astype(BF16)
    return hi, lo


def _dot(a, b):
    return jnp.dot(a, b, preferred_element_type=F32)


def _dot_hp(a, b):
    a_hi, a_lo = _split(a)
    b_hi, b_lo = _split(b)
    return _dot(a_hi, b_hi) + (_dot(a_hi, b_lo) + _dot(a_lo, b_hi))


def _sigmoid(x):
    return 1.0 / (1.0 + jnp.exp(-x))


def _gelu_tanh(x):
    c = math.sqrt(2.0 / math.pi)
    return 0.5 * x * (1.0 + jnp.tanh(c * (x + 0.044715 * (x * x * x))))


def _rms_scale(x):
    return x * jax.lax.rsqrt(jnp.mean(x * x, axis=-1, keepdims=True) + NORM_EPS)


def _mod_kernel(c_ref, w_ref, b_ref, o_ref):
    c = c_ref[...]
    s = c * _sigmoid(c)
    o_ref[...] = _dot_hp(s, w_ref[...]) + b_ref[...]


def _modulation(c, w_ada, b_ada):
    d, n = w_ada.shape
    bn = n // 4
    c8 = jnp.broadcast_to(c, (8, d))
    out = pl.pallas_call(
        _mod_kernel,
        grid=(n // bn,),
        in_specs=[pl.BlockSpec((8, d), lambda j: (0, 0)),
                  pl.BlockSpec((d, bn), lambda j: (0, j)),
                  pl.BlockSpec((1, bn), lambda j: (0, j))],
        out_specs=pl.BlockSpec((8, bn), lambda j: (0, j)),
        out_shape=jax.ShapeDtypeStruct((8, n), F32),
        compiler_params=_params(1),
        name="mod",
    )(c8, w_ada, b_ada.reshape(1, n))
    return out[0:1]


def _inproj_kernel(x_ref, g_ref, sc_ref, sh_ref, cos_ref, sin_ref,
                   wq_ref, wqr_ref, wk_ref, wkr_ref, wv_ref, wg_ref, ws_ref,
                   qT_ref, kc_ref, ks_ref, kw_ref, vc_ref, vsT_ref, vwT_ref, gT_ref, zs_ref):
    x = x_ref[...]
    h = _rms_scale(x) * g_ref[...] * (1.0 + sc_ref[...]) + sh_ref[...]
    hb = h.astype(BF16)
    cos = cos_ref[...]
    sin = sin_ref[...]
    n_sub = x.shape[0] // Q_TILE

    zq = _dot(hb, wq_ref[...])
    zqr = _dot(hb, wqr_ref[...])
    scale = HEAD_DIM ** -0.5
    q_cols = []
    for j in range(ATTN_WIDTH // LANES):
        sl = slice(j * LANES, (j + 1) * LANES)
        q_cols.append((zq[:, sl] * cos + zqr[:, sl] * sin) * scale)
    q = jnp.concatenate(q_cols, axis=1)
    for s in range(n_sub):
        qs = q[s * Q_TILE:(s + 1) * Q_TILE, :].T
        qT_ref[s] = qs.reshape(KV_HEADS, Q_PER_KV * HEAD_DIM, Q_TILE)

    zk = _dot(hb, wk_ref[...])
    zkr = _dot(hb, wkr_ref[...])
    k_outs = (kc_ref, ks_ref, kw_ref)
    for j in range(3):
        sl = slice(j * LANES, (j + 1) * LANES)
        kj = zk[:, sl] * cos + zkr[:, sl] * sin
        k_outs[j][...] = kj.astype(k_outs[j].dtype)

    zv = _dot(hb, wv_ref[...])
    vc_ref[...] = zv[:, 0:LANES]
    vsT_ref[0] = zv[:, LANES:2 * LANES].T.astype(BF16)
    vw = zv[:, 2 * LANES:3 * LANES]
    for s in range(n_sub):
        vwT_ref[s] = vw[s * Q_TILE:(s + 1) * Q_TILE, :].T.astype(BF16)

    gates = _sigmoid(_dot(hb, wg_ref[...]))
    for s in range(n_sub):
        gt = gates[s * Q_TILE:(s + 1) * Q_TILE, :].T
        gT_ref[s] = gt[0:2 * 16, :].reshape(KV_HEADS, 16, Q_TILE)

    zs_ref[...] = _dot(hb, ws_ref[...])


def _rot_cols(w):
    d, n = w.shape
    w4 = w.reshape(d, n // HEAD_DIM, 2, HEAD_DIM // 2)
    return jnp.concatenate([-w4[:, :, 1:2], w4[:, :, 0:1]], axis=2).reshape(d, n)


def _in_projection(x2, norm_g, sc, sh, w_in):
    l, d = x2.shape
    tm = ROW_TILE
    cuts = np.cumsum([ATTN_WIDTH] + [KV_WIDTH] * 6 + [3 * ATTN_HEADS])
    wq = w_in[:, :cuts[0]]
    wkc, wvc = w_in[:, cuts[0]:cuts[1]], w_in[:, cuts[1]:cuts[2]]
    wks, wvs = w_in[:, cuts[2]:cuts[3]], w_in[:, cuts[3]:cuts[4]]
    wkw, wvw = w_in[:, cuts[4]:cuts[5]], w_in[:, cuts[5]:cuts[6]]
    wg = w_in[:, cuts[6]:cuts[7]]
    ws = w_in[:, cuts[7]:]
    ssm_w = ws.shape[1]
    wk = jnp.concatenate([wkc, wks, wkw], axis=1)
    wv = jnp.concatenate([wvc, wvs, wvw], axis=1)
    wg3 = wg.reshape(d, KV_HEADS, Q_PER_KV * 3)
    wg3 = jnp.pad(wg3, ((0, 0), (0, 0), (0, 16 - Q_PER_KV * 3))).reshape(d, KV_HEADS * 16)
    wgp = jnp.pad(wg3, ((0, 0), (0, LANES - KV_HEADS * 16)))
    weights = [w.astype(BF16) for w in (wq, _rot_cols(wq), wk, _rot_cols(wk), wv, wgp, ws)]

    half = HEAD_DIM // 2
    inv = jnp.power(ROPE_THETA, -jnp.arange(half, dtype=F32) * 2.0 / HEAD_DIM)
    ang = jnp.arange(l, dtype=F32)[:, None] * inv[None, :]
    cos = jnp.tile(jnp.cos(ang), (1, LANES // half))
    sin = jnp.tile(jnp.sin(ang), (1, LANES // half))

    n_q = l // Q_TILE
    sub = tm // Q_TILE
    row = lambda i: (i, 0)
    const = lambda i: (0, 0)
    out_shape = [
        jax.ShapeDtypeStruct((n_q, KV_HEADS, Q_PER_KV * HEAD_DIM, Q_TILE), F32),
        jax.ShapeDtypeStruct((l, KV_WIDTH), F32),
        jax.ShapeDtypeStruct((l, KV_WIDTH), BF16),
        jax.ShapeDtypeStruct((l, KV_WIDTH), BF16),
        jax.ShapeDtypeStruct((l, KV_WIDTH), F32),
        jax.ShapeDtypeStruct((l // tm, KV_WIDTH, tm), BF16),
        jax.ShapeDtypeStruct((n_q, KV_WIDTH, Q_TILE), BF16),
        jax.ShapeDtypeStruct((n_q, KV_HEADS, 16, Q_TILE), F32),
        jax.ShapeDtypeStruct((l, ssm_w), F32),
    ]
    out_specs = [
        pl.BlockSpec((sub, KV_HEADS, Q_PER_KV * HEAD_DIM, Q_TILE), lambda i: (i, 0, 0, 0)),
        pl.BlockSpec((tm, KV_WIDTH), row),
        pl.BlockSpec((tm, KV_WIDTH), row),
        pl.BlockSpec((tm, KV_WIDTH), row),
        pl.BlockSpec((tm, KV_WIDTH), row),
        pl.BlockSpec((1, KV_WIDTH, tm), lambda i: (i, 0, 0)),
        pl.BlockSpec((sub, KV_WIDTH, Q_TILE), lambda i: (i, 0, 0)),
        pl.BlockSpec((sub, KV_HEADS, 16, Q_TILE), lambda i: (i, 0, 0, 0)),
        pl.BlockSpec((tm, ssm_w), row),
    ]
    in_specs = [pl.BlockSpec((tm, d), row),
                pl.BlockSpec((1, d), const), pl.BlockSpec((1, d), const), pl.BlockSpec((1, d), const),
                pl.BlockSpec((tm, LANES), row), pl.BlockSpec((tm, LANES), row)]
    in_specs += [pl.BlockSpec(w.shape, const) for w in weights]
    return pl.pallas_call(
        _inproj_kernel,
        grid=(l // tm,),
        in_specs=in_specs,
        out_specs=out_specs,
        out_shape=out_shape,
        compiler_params=_params(1),
        name="inproj",
    )(x2, norm_g.reshape(1, d), sc, sh, cos, sin, *weights)


def _compress_kernel(x_ref, pe_ref, w1_ref, w2_ref, o_ref, *, transpose_out):
    x = x_ref[0]
    half = x.shape[1]
    w1 = w1_ref[...]
    first = _dot_hp(x, w1[0:half])
    second = _dot_hp(x, w1[half:2 * half])
    bias = _dot_hp(pe_ref[...], w1)[0:1]
    n = x.shape[0]
    h1 = first + pltpu.roll(second, n - 1, 0) + bias
    out = _dot_hp(_gelu_tanh(h1), w2_ref[...])
    if transpose_out:
        o_ref[0] = out.T[0:HEAD_DIM, :]
    else:
        o_ref[0] = out[:, 0:HEAD_DIM]


def _compress(z, pe, w1, w2, transpose_out):
    l = z.shape[0]
    n = l // CMP_STRIDE
    x16 = z.reshape(n, CMP_STRIDE, KV_HEADS, HEAD_DIM).transpose(2, 0, 1, 3)
    x16 = x16.reshape(KV_HEADS, n, CMP_STRIDE * HEAD_DIM)
    pe8 = jnp.broadcast_to(pe.reshape(1, CMP_BLOCK * HEAD_DIM), (8, CMP_BLOCK * HEAD_DIM))
    w2p = jnp.pad(w2, ((0, 0), (0, LANES - HEAD_DIM)))
    oshape = (KV_HEADS, HEAD_DIM, n) if transpose_out else (KV_HEADS, n, HEAD_DIM)
    return pl.pallas_call(
        functools.partial(_compress_kernel, transpose_out=transpose_out),
        grid=(KV_HEADS,),
        in_specs=[pl.BlockSpec((1, n, CMP_STRIDE * HEAD_DIM), lambda h: (h, 0, 0)),
                  pl.BlockSpec(pe8.shape, lambda h: (0, 0)),
                  pl.BlockSpec(w1.shape, lambda h: (0, 0)),
                  pl.BlockSpec(w2p.shape, lambda h: (0, 0))],
        out_specs=pl.BlockSpec((1,) + oshape[1:], lambda h: (h, 0, 0)),
        out_shape=jax.ShapeDtypeStruct(oshape, F32),
        compiler_params=_params(1),
        name="compress_v" if transpose_out else "compress_k",
    )(x16, pe8, w1, w2p)


def _tile4(a):
    return jnp.concatenate([a] * Q_PER_KV, axis=1)


def _attn_kernel(qT_ref, kc_ref, vcT_ref, ovl_ref, ks_ref, vsT_ref, *rest):
    kw_refs = rest[0:WIN_CHUNKS]
    vw_refs = rest[WIN_CHUNKS:2 * WIN_CHUNKS]
    gT_ref, o_ref, sel_ref = rest[2 * WIN_CHUNKS:]
    h = pl.program_id(0)
    i = pl.program_id(1)
    n_cmp = kc_ref.shape[1]
    n_sel = ovl_ref.shape[0]
    G, DH, TQ = Q_PER_KV, HEAD_DIM, Q_TILE

    qT = qT_ref[0, 0]
    q_all = jnp.concatenate([qT[g * DH:(g + 1) * DH, :] for g in range(G)], axis=1)
    zeros = jnp.zeros_like(q_all)
    q_pad = jnp.where(h == 0, jnp.concatenate([q_all, zeros], axis=0),
                      jnp.concatenate([zeros, q_all], axis=0)).astype(BF16)
    head_rows = pl.ds(pl.multiple_of(h * DH, DH), DH)

    t = i * TQ + jax.lax.broadcasted_iota(jnp.int32, (1, TQ), 1)

    kc = kc_ref[0]
    vcT = vcT_ref[0].astype(BF16)
    c_end = jax.lax.broadcasted_iota(jnp.int32, (n_cmp, 1), 0) * CMP_STRIDE + (CMP_BLOCK - 1)
    mask_c = c_end <= t
    p_sum = jnp.zeros((n_cmp, TQ), F32)
    o_c = []
    for g in range(G):
        s = _dot_hp(kc, qT[g * DH:(g + 1) * DH, :])
        s = jnp.where(mask_c, s, NEG_INF)
        m = jnp.max(s, axis=0, keepdims=True)
        m = jnp.where(m > NEG_INF, m, 0.0)
        e = jnp.where(mask_c, jnp.exp(s - m), 0.0)
        p = e / jnp.maximum(jnp.sum(e, axis=0, keepdims=True), 1e-30)
        p_sum = p_sum + p
        o_c.append(_dot(vcT, p.astype(BF16)))

    p_hi, p_lo = _split(p_sum)
    ovl = ovl_ref[...]
    imp = _dot(ovl, p_hi) + _dot(ovl, p_lo)
    blk = jax.lax.broadcasted_iota(jnp.int32, (n_sel, 1), 0)
    blk_f = blk.astype(F32)
    cur = jax.lax.shift_right_logical(t, int(math.log2(SEL_BLOCK)))
    forced = (blk == 0) | (blk == cur) | (blk == cur - 1)
    score = jnp.where(blk > cur, NEG_INF, jnp.where(forced, FORCE_SCORE, imp))

    def pick_one(_, carry):
        score, sel = carry
        m = jnp.max(score, axis=0, keepdims=True)
        first = jnp.min(jnp.where(score == m, blk_f, float(n_sel)), axis=0, keepdims=True)
        hit = blk_f == first
        sel = jnp.where(hit & (m > NEG_INF), 1.0, sel)
        return jnp.where(hit, NEG_INF, score), sel

    _, sel = jax.lax.fori_loop(0, min(SEL_TOPN, n_sel), pick_one,
                               (score, jnp.zeros((n_sel, TQ), F32)))
    sel_ref[...] = sel

    blocks_per_chunk = KEY_CHUNK // SEL_BLOCK
    row_k = jax.lax.broadcasted_iota(jnp.int32, (KEY_CHUNK, 1), 0)

    def sel_chunk(c, carry):
        m_run, l_run, acc = carry
        k = ks_ref[pl.ds(pl.multiple_of(c * KEY_CHUNK, KEY_CHUNK), KEY_CHUNK), :]
        s = _dot(k, q_pad)
        sel_rows = sel_ref[pl.ds(pl.multiple_of(c * blocks_per_chunk, blocks_per_chunk),
                                 blocks_per_chunk), :]
        chosen = jnp.concatenate(
            [jnp.broadcast_to(sel_rows[b:b + 1, :], (SEL_BLOCK, TQ)) for b in range(blocks_per_chunk)],
            axis=0)
        valid = _tile4((chosen > 0.5) & ((c * KEY_CHUNK + row_k) <= t))
        m_new = jnp.maximum(m_run, jnp.max(jnp.where(valid, s, -1e30), axis=0, keepdims=True))
        p = jnp.where(valid, jnp.exp(s - m_new), 0.0)
        alpha = jnp.exp(m_run - m_new)
        l_new = alpha * l_run + jnp.sum(p, axis=0, keepdims=True)
        vT = vsT_ref[c, head_rows, :]
        acc = alpha * acc + _dot(vT, p.astype(BF16))
        return m_new, l_new, acc

    n_chunks = (i * TQ + TQ - 1) // KEY_CHUNK + 1
    init = (jnp.full((1, G * TQ), -1e30, F32), jnp.zeros((1, G * TQ), F32), jnp.zeros((DH, G * TQ), F32))
    _, l_s, acc_s = jax.lax.fori_loop(0, n_chunks, sel_chunk, init)
    o_s = acc_s / l_s

    k_w = jnp.concatenate([r[...] for r in kw_refs], axis=0)
    v_wT = jnp.concatenate([r[0, head_rows, :] for r in vw_refs], axis=1)
    s = _dot(k_w, q_pad)
    w_pos = (i - (WIN_CHUNKS - 1)) * TQ + jax.lax.broadcasted_iota(jnp.int32, (WIN_CHUNKS * TQ, 1), 0)
    valid = _tile4((w_pos <= t) & (w_pos > t - WINDOW) & (w_pos >= 0))
    m = jnp.max(jnp.where(valid, s, -1e30), axis=0, keepdims=True)
    p = jnp.where(valid, jnp.exp(s - m), 0.0)
    l_w = jnp.sum(p, axis=0, keepdims=True)
    o_w = _dot(v_wT, p.astype(BF16)) / l_w

    gates = gT_ref[0, 0]
    outs = []
    for g in range(G):
        cols = slice(g * TQ, (g + 1) * TQ)
        outs.append(gates[3 * g:3 * g + 1, :] * o_c[g]
                    + gates[3 * g + 1:3 * g + 2, :] * o_s[:, cols]
                    + gates[3 * g + 2:3 * g + 3, :] * o_w[:, cols])
    o_ref[...] = jnp.concatenate(outs, axis=0).T


def _attention(qT, kc, vcT, ks, vsT, kw, vwT, gT, l):
    n_q = l // Q_TILE
    n_cmp = l // CMP_STRIDE
    n_sel = l // SEL_BLOCK
    c_start = np.arange(n_cmp) * CMP_STRIDE
    s_start = np.arange(n_sel) * SEL_BLOCK
    ovl = np.clip(np.minimum(c_start[None, :] + CMP_BLOCK, s_start[:, None] + SEL_BLOCK)
                  - np.maximum(c_start[None, :], s_start[:, None]), 0, None) / CMP_BLOCK
    ovl[:, n_cmp - 1] = 0.0
    ovl = jnp.asarray(ovl, BF16)

    def win(j):
        return lambda h, i: (jnp.maximum(i - (WIN_CHUNKS - 1) + j, 0), 0)

    def win3(j):
        return lambda h, i: (jnp.maximum(i - (WIN_CHUNKS - 1) + j, 0), 0, 0)

    in_specs = [
        pl.BlockSpec((1, 1, Q_PER_KV * HEAD_DIM, Q_TILE), lambda h, i: (i, h, 0, 0)),
        pl.BlockSpec((1, n_cmp, HEAD_DIM), lambda h, i: (h, 0, 0)),
        pl.BlockSpec((1, HEAD_DIM, n_cmp), lambda h, i: (h, 0, 0)),
        pl.BlockSpec(ovl.shape, lambda h, i: (0, 0)),
        pl.BlockSpec(ks.shape, lambda h, i: (0, 0)),
        pl.BlockSpec(vsT.shape, lambda h, i: (0, 0, 0)),
    ]
    in_specs += [pl.BlockSpec((Q_TILE, KV_WIDTH), win(j)) for j in range(WIN_CHUNKS)]
    in_specs += [pl.BlockSpec((1, KV_WIDTH, Q_TILE), win3(j)) for j in range(WIN_CHUNKS)]
    in_specs += [pl.BlockSpec((1, 1, 16, Q_TILE), lambda h, i: (i, h, 0, 0))]
    return pl.pallas_call(
        _attn_kernel,
        grid=(KV_HEADS, n_q),
        in_specs=in_specs,
        out_specs=pl.BlockSpec((Q_TILE, Q_PER_KV * HEAD_DIM), lambda h, i: (i, h)),
        out_shape=jax.ShapeDtypeStruct((l, ATTN_WIDTH), F32),
        scratch_shapes=[pltpu.VMEM((n_sel, Q_TILE), F32)],
        compiler_params=_params(2),
        name="attn",
    )(qT, kc, vcT, ovl, ks, vsT, *([kw] * WIN_CHUNKS), *([vwT] * WIN_CHUNKS), gT)


SSM_TILE = 128
SSM_PACK = 16


def _ssm_kernel(u_ref, a_ref, wb_ref, wc_ref, msk_ref, d_ref, wg_ref, bg_ref, y_ref,
                urow_scr, x_scr, yall_scr, y_scr, st_scr):
    step = pl.program_id(0)
    tc, width = u_ref.shape
    P = SSM_PACK

    @pl.when(step == 0)
    def _():
        st_scr[...] = jnp.zeros_like(st_scr)

    msk = msk_ref[...]

    def spread(ti, _):
        rows = pl.ds(pl.multiple_of(ti * P, P), P)
        urow_scr[rows, :] = (jnp.broadcast_to(u_ref[pl.ds(ti, 1), :], (P, width)) * msk).astype(BF16)
        return 0

    jax.lax.fori_loop(0, tc, spread, 0, unroll=8)
    x_scr[...] = _dot(urow_scr[...], wb_ref[...])

    a_re = a_ref[0]
    a_im = a_ref[1]

    def one(ti, carry):
        x_re, x_im = carry
        rows = pl.ds(pl.multiple_of(ti * P, P), P)
        n_re = a_re * x_re - a_im * x_im + x_scr[rows, 0:LANES]
        n_im = a_re * x_im + a_im * x_re + x_scr[rows, LANES:2 * LANES]
        x_scr[rows, 0:LANES] = n_re
        x_scr[rows, LANES:2 * LANES] = n_im
        return n_re, n_im

    x_re, x_im = jax.lax.fori_loop(0, tc, one, (st_scr[0], st_scr[1]), unroll=8)
    st_scr[0] = x_re
    st_scr[1] = x_im

    yall_scr[...] = _dot(x_scr[...].astype(BF16), wc_ref[...])

    def gather(ti, _):
        rows = pl.ds(pl.multiple_of(ti * P, P), P)
        y_scr[pl.ds(ti, 1), :] = jnp.sum(yall_scr[rows, :] * msk, axis=0, keepdims=True)
        return 0

    jax.lax.fori_loop(0, tc, gather, 0, unroll=8)
    u = u_ref[...]
    y = _gelu_tanh(y_scr[...] + d_ref[...] * u)
    y_ref[...] = y * _sigmoid(_dot(y.astype(BF16), wg_ref[...]) + bg_ref[...])


def _s5(zs, lam_re, lam_im, log_step, b_re, b_im, c_re, c_im, d_skip, w_glu, b_glu):
    l, width = zs.shape
    G, Pn, C = b_re.shape
    assert Pn == SSM_STATE and C == SSM_GROUP and G * C == width and G == 2 * SSM_PACK
    step = jnp.exp(log_step)[:, None]
    mag = jnp.exp(lam_re * step)
    a_re = mag * jnp.cos(lam_im * step)
    a_im = mag * jnp.sin(lam_im * step)
    den = lam_re * lam_re + lam_im * lam_im
    n_re = a_re - 1.0
    f_re = (n_re * lam_re + a_im * lam_im) / den
    f_im = (a_im * lam_re - n_re * lam_im) / den
    bb_re = f_re[..., None] * b_re - f_im[..., None] * b_im
    bb_im = f_re[..., None] * b_im + f_im[..., None] * b_re
    a_pack = jnp.stack([a_re.reshape(SSM_PACK, 2 * Pn), a_im.reshape(SSM_PACK, 2 * Pn)])

    eye2 = jnp.eye(2, dtype=F32)

    def b_stack(bb):
        t = bb.reshape(SSM_PACK, 2, Pn, C)
        return jnp.einsum('jgpc,gh->jhcgp', t, eye2).reshape(width, 2 * Pn)

    w_b = jnp.concatenate([b_stack(bb_re), b_stack(bb_im)], axis=1).astype(BF16)

    def c_stack(cc):
        t = cc.reshape(SSM_PACK, 2, C, Pn)
        return jnp.einsum('jgcp,gh->gpjhc', t, eye2).reshape(2 * Pn, width)

    w_c = jnp.concatenate([c_stack(c_re), -c_stack(c_im)], axis=0).astype(BF16)
    lane_row = jnp.arange(width) // (2 * C)
    msk = (lane_row[None, :] == jnp.arange(SSM_PACK)[:, None]).astype(F32)

    tc = SSM_TILE
    const2 = lambda i: (0, 0)
    return pl.pallas_call(
        _ssm_kernel,
        grid=(l // tc,),
        in_specs=[pl.BlockSpec((tc, width), lambda i: (i, 0)),
                  pl.BlockSpec(a_pack.shape, lambda i: (0, 0, 0)),
                  pl.BlockSpec(w_b.shape, const2),
                  pl.BlockSpec(w_c.shape, const2),
                  pl.BlockSpec(msk.shape, const2),
                  pl.BlockSpec((1, width), const2),
                  pl.BlockSpec((width, width), const2),
                  pl.BlockSpec((1, width), const2)],
        out_specs=pl.BlockSpec((tc, width), lambda i: (i, 0)),
        out_shape=jax.ShapeDtypeStruct((l, width), F32),
        scratch_shapes=[pltpu.VMEM((tc * SSM_PACK, width), BF16),
                        pltpu.VMEM((tc * SSM_PACK, 2 * LANES), F32),
                        pltpu.VMEM((tc * SSM_PACK, width), F32),
                        pltpu.VMEM((tc, width), F32),
                        pltpu.VMEM((2, SSM_PACK, LANES), F32)],
        compiler_params=_params(1),
        name="ssm",
    )(zs, a_pack, w_b, w_c, msk, d_skip.reshape(1, width), w_glu.astype(BF16), b_glu.reshape(1, width))


def _outproj_kernel(x_ref, oa_ref, ys_ref, ga_ref, gs_ref, wa_ref, ws_ref, gt_ref, o_ref):
    a = (_rms_scale(oa_ref[...]) * ga_ref[...]).astype(BF16)
    s = (_rms_scale(ys_ref[...]) * gs_ref[...]).astype(BF16)
    y = _dot(a, wa_ref[...]) + _dot(s, ws_ref[...])
    o_ref[...] = x_ref[...] + gt_ref[...] * y


def _out_projection(x2, o_attn, y_ssm, attn_g, ssm_g, w_o, gt):
    l, d = x2.shape
    tm = ROW_TILE
    aw, sw = o_attn.shape[1], y_ssm.shape[1]
    row = lambda i: (i, 0)
    const = lambda i: (0, 0)
    return pl.pallas_call(
        _outproj_kernel,
        grid=(l // tm,),
        in_specs=[pl.BlockSpec((tm, d), row), pl.BlockSpec((tm, aw), row), pl.BlockSpec((tm, sw), row),
                  pl.BlockSpec((1, aw), const), pl.BlockSpec((1, sw), const),
                  pl.BlockSpec((aw, d), const), pl.BlockSpec((sw, d), const), pl.BlockSpec((1, d), const)],
        out_specs=pl.BlockSpec((tm, d), row),
        out_shape=jax.ShapeDtypeStruct((l, d), F32),
        compiler_params=_params(1),
        name="outproj",
    )(x2, o_attn, y_ssm, attn_g.reshape(1, aw), ssm_g.reshape(1, sw),
      w_o[:aw].astype(BF16), w_o[aw:].astype(BF16), gt)


FFN_TILE = 256
FFN_COLS = 256
HALO = 8


def _ffn_kernel(x_ref, g_ref, sc_ref, sh_ref, wup_ref, cw_ref, cb_ref, wdn_ref, gt_ref, gf_ref,
                o_ref, ua_scr, uv_scr):
    step = pl.program_id(0)
    tm = x_ref.shape[0]
    n_col = wdn_ref.shape[0]

    @pl.when(step == 0)
    def _():
        ua_scr[...] = jnp.zeros_like(ua_scr)
        uv_scr[...] = jnp.zeros_like(uv_scr)

    x = x_ref[...]
    hb = (_rms_scale(x) * g_ref[...] * (1.0 + sc_ref[...]) + sh_ref[...]).astype(BF16)

    def conv(scr, j, u, col):
        scr[j, 0:HALO, :] = scr[j, tm:tm + HALO, :]
        scr[j, HALO:tm + HALO, :] = u
        w = cw_ref[col]
        out = cb_ref[col]
        for k in range(CONV_WIDTH):
            off = HALO - (CONV_WIDTH - 1) + k
            out = out + w[k:k + 1, :] * scr[j, off:off + tm, :]
        return out

    def col_step(j, acc):
        a = conv(ua_scr, j, _dot(hb, wup_ref[j]), j)
        v = conv(uv_scr, j, _dot(hb, wup_ref[n_col + j]), n_col + j)
        act = (a * _sigmoid(a) * v).astype(BF16)
        return acc + _dot(act, wdn_ref[j])

    y = jax.lax.fori_loop(0, n_col, col_step, jnp.zeros(x.shape, F32))
    x2 = x + gt_ref[...] * y
    o_ref[...] = _rms_scale(x2) * gf_ref[...]


def _conv_ffn(x1, norm_g, sc, sh, w_up, conv_w, conv_b, w_down, gt, normf_g):
    l, d = x1.shape
    f = w_down.shape[0]
    tm, cw = FFN_TILE, FFN_COLS
    n_col = f // cw
    wup = w_up.astype(BF16).reshape(d, 2 * n_col, cw).transpose(1, 0, 2)
    cwt = conv_w.reshape(CONV_WIDTH, 2 * n_col, cw).transpose(1, 0, 2)
    cbt = conv_b.reshape(2 * n_col, 1, cw)
    wdn = w_down.astype(BF16).reshape(n_col, cw, d)
    row = lambda i: (i, 0)
    const = lambda i: (0, 0)
    const3 = lambda i: (0, 0, 0)
    return pl.pallas_call(
        _ffn_kernel,
        grid=(l // tm,),
        in_specs=[pl.BlockSpec((tm, d), row),
                  pl.BlockSpec((1, d), const), pl.BlockSpec((1, d), const), pl.BlockSpec((1, d), const),
                  pl.BlockSpec(wup.shape, const3), pl.BlockSpec(cwt.shape, const3),
                  pl.BlockSpec(cbt.shape, const3), pl.BlockSpec(wdn.shape, const3),
                  pl.BlockSpec((1, d), const), pl.BlockSpec((1, d), const)],
        out_specs=pl.BlockSpec((tm, d), row),
        out_shape=jax.ShapeDtypeStruct((l, d), F32),
        scratch_shapes=[pltpu.VMEM((n_col, tm + HALO, cw), F32),
                        pltpu.VMEM((n_col, tm + HALO, cw), F32)],
        compiler_params=_params(1),
        name="ffn",
    )(x1, norm_g.reshape(1, d), sc, sh, wup, cwt, cbt, wdn, gt, normf_g.reshape(1, d))


def kernel(x, c, norm1_g, norm2_g, normf_g, w_ada, b_ada, w_in, pe_k, pe_v, w_ck1, w_ck2, w_cv1, w_cv2,
           lam_re, lam_im, log_step, b_re, b_im, c_re, c_im, d_skip, w_glu, b_glu, attn_norm_g,
           ssm_norm_g, w_o, w_up, conv_w, conv_b, w_down):
    b, l, d = x.shape
    depth = w_ada.shape[0]
    assert b == 1 and depth == 1
    assert l % KEY_CHUNK == 0 and l % ROW_TILE == 0 and l >= WINDOW
    x2 = x.reshape(l, d)
    mod = _modulation(c, w_ada[0], b_ada[0])
    sh1, sc1, gt1, sh2, sc2, gt2 = [mod[:, k * d:(k + 1) * d] for k in range(6)]

    qT, k_c, k_s, k_w, v_c, v_sT, v_wT, gT, zs = _in_projection(x2, norm1_g[0], sc1, sh1, w_in[0])
    kc = _compress(k_c, pe_k[0], w_ck1[0], w_ck2[0], transpose_out=False)
    vcT = _compress(v_c, pe_v[0], w_cv1[0], w_cv2[0], transpose_out=True)
    o_attn = _attention(qT, kc, vcT, k_s, v_sT, k_w, v_wT, gT, l)
    y_ssm = _s5(zs, lam_re[0], lam_im[0], log_step[0], b_re[0], b_im[0], c_re[0], c_im[0],
                d_skip[0], w_glu[0], b_glu[0])
    x1 = _out_projection(x2, o_attn, y_ssm, attn_norm_g[0], ssm_norm_g[0], w_o[0], gt1)
    out = _conv_ffn(x1, norm2_g[0], sc2, sh2, w_up[0], conv_w[0], conv_b[0], w_down[0], gt2, normf_g)
    return out.reshape(b, l, d)
```

```python
import functools
import math

import jax
import jax.numpy as jnp
import numpy as np
from jax.experimental import pallas as pl
from jax.experimental.pallas import tpu as pltpu

ATTN_HEADS = 8
KV_HEADS = 2
HEAD_DIM = 64
Q_PER_KV = ATTN_HEADS // KV_HEADS
ATTN_WIDTH = ATTN_HEADS * HEAD_DIM
KV_WIDTH = KV_HEADS * HEAD_DIM
SSM_GROUP = 16
SSM_STATE = 64
CMP_BLOCK = 32
CMP_STRIDE = 16
SEL_BLOCK = 64
SEL_TOPN = 16
WINDOW = 512
FORCE_SCORE = 1.0e4
ROPE_THETA = 10000.0
CONV_WIDTH = 3
NORM_EPS = 1e-6

LANES = 128
Q_TILE = 128
KEY_CHUNK = 512
ROW_TILE = 512
WIN_CHUNKS = WINDOW // Q_TILE + 1
VMEM_LIMIT = 56 * 1024 * 1024

F32 = jnp.float32
BF16 = jnp.bfloat16
NEG_INF = float("-inf")


def _params(n_axes, vmem=VMEM_LIMIT):
    return pltpu.CompilerParams(dimension_semantics=("arbitrary",) * n_axes,
                                vmem_limit_bytes=vmem)


def _split(a):
    hi = a.astype(BF16)
    lo = (a - hi.astype(F32)).astype(BF16)
    return hi, lo


def _dot(a, b):
    return jnp.dot(a, b, preferred_element_type=F32)


def _dot_hp(a, b):
    a_hi, a_lo = _split(a)
    b_hi, b_lo = _split(b)
    return _dot(a_hi, b_hi) + (_dot(a_hi, b_lo) + _dot(a_lo, b_hi))


def _sigmoid(x):
    return 1.0 / (1.0 + jnp.exp(-x))


def _gelu_tanh(x):
    c = math.sqrt(2.0 / math.pi)
    return 0.5 * x * (1.0 + jnp.tanh(c * (x + 0.044715 * (x * x * x))))


def _rms_scale(x):
    return x * jax.lax.rsqrt(jnp.mean(x * x, axis=-1, keepdims=True) + NORM_EPS)


def _mod_kernel(c_ref, w_ref, b_ref, o_ref):
    c = c_ref[...]
    s = c * _sigmoid(c)
    o_ref[...] = _dot_hp(s, w_ref[...]) + b_ref[...]


def _modulation(c, w_ada, b_ada):
    d, n = w_ada.shape
    bn = n // 4
    c8 = jnp.broadcast_to(c, (8, d))
    out = pl.pallas_call(
        _mod_kernel,
        grid=(n // bn,),
        in_specs=[pl.BlockSpec((8, d), lambda j: (0, 0)),
                  pl.BlockSpec((d, bn), lambda j: (0, j)),
                  pl.BlockSpec((1, bn), lambda j: (0, j))],
        out_specs=pl.BlockSpec((8, bn), lambda j: (0, j)),
        out_shape=jax.ShapeDtypeStruct((8, n), F32),
        compiler_params=_params(1),
        name="mod",
    )(c8, w_ada, b_ada.reshape(1, n))
    return out[0:1]


def _inproj_kernel(x_ref, g_ref, sc_ref, sh_ref, cos_ref, sin_ref,
                   wq_ref, wqr_ref, wk_ref, wkr_ref, wv_ref, wg_ref, ws_ref,
                   qT_ref, kc_ref, ks_ref, kw_ref, vc_ref, vsT_ref, vwT_ref, gT_ref, zs_ref):
    x = x_ref[...]
    h = _rms_scale(x) * g_ref[...] * (1.0 + sc_ref[...]) + sh_ref[...]
    hb = h.astype(BF16)
    cos = cos_ref[...]
    sin = sin_ref[...]
    n_sub = x.shape[0] // Q_TILE

    zq = _dot(hb, wq_ref[...])
    zqr = _dot(hb, wqr_ref[...])
    scale = HEAD_DIM ** -0.5 * math.log2(math.e)
    q_cols = []
    for j in range(ATTN_WIDTH // LANES):
        sl = slice(j * LANES, (j + 1) * LANES)
        q_cols.append((zq[:, sl] * cos + zqr[:, sl] * sin) * scale)
    q = jnp.concatenate(q_cols, axis=1)
    for s in range(n_sub):
        qs = q[s * Q_TILE:(s + 1) * Q_TILE, :].T
        qT_ref[s] = qs.reshape(KV_HEADS, Q_PER_KV * HEAD_DIM, Q_TILE)

    zk = _dot(hb, wk_ref[...])
    zkr = _dot(hb, wkr_ref[...])
    k_outs = (kc_ref, ks_ref, kw_ref)
    for j in range(3):
        sl = slice(j * LANES, (j + 1) * LANES)
        kj = zk[:, sl] * cos + zkr[:, sl] * sin
        k_outs[j][...] = kj.astype(k_outs[j].dtype)

    zv = _dot(hb, wv_ref[...])
    vc_ref[...] = zv[:, 0:LANES]
    vs = zv[:, LANES:2 * LANES]
    vw = zv[:, 2 * LANES:3 * LANES]
    for s in range(n_sub):
        vsT_ref[s] = vs[s * Q_TILE:(s + 1) * Q_TILE, :].T.astype(BF16)
        vwT_ref[s] = vw[s * Q_TILE:(s + 1) * Q_TILE, :].T.astype(BF16)

    gates = _sigmoid(_dot(hb, wg_ref[...]))
    for s in range(n_sub):
        gt = gates[s * Q_TILE:(s + 1) * Q_TILE, :].T
        gT_ref[s] = gt[0:2 * 16, :].reshape(KV_HEADS, 16, Q_TILE)

    zs_ref[...] = _dot(hb, ws_ref[...])


def _rot_cols(w):
    d, n = w.shape
    w4 = w.reshape(d, n // HEAD_DIM, 2, HEAD_DIM // 2)
    return jnp.concatenate([-w4[:, :, 1:2], w4[:, :, 0:1]], axis=2).reshape(d, n)


def _in_projection(x2, norm_g, sc, sh, w_in):
    l, d = x2.shape
    tm = ROW_TILE
    cuts = np.cumsum([ATTN_WIDTH] + [KV_WIDTH] * 6 + [3 * ATTN_HEADS])
    wq = w_in[:, :cuts[0]]
    wkc, wvc = w_in[:, cuts[0]:cuts[1]], w_in[:, cuts[1]:cuts[2]]
    wks, wvs = w_in[:, cuts[2]:cuts[3]], w_in[:, cuts[3]:cuts[4]]
    wkw, wvw = w_in[:, cuts[4]:cuts[5]], w_in[:, cuts[5]:cuts[6]]
    wg = w_in[:, cuts[6]:cuts[7]]
    ws = w_in[:, cuts[7]:]
    ssm_w = ws.shape[1]
    wk = jnp.concatenate([wkc, wks, wkw], axis=1)
    wv = jnp.concatenate([wvc, wvs, wvw], axis=1)
    wg3 = wg.reshape(d, KV_HEADS, Q_PER_KV * 3)
    wg3 = jnp.pad(wg3, ((0, 0), (0, 0), (0, 16 - Q_PER_KV * 3))).reshape(d, KV_HEADS * 16)
    wgp = jnp.pad(wg3, ((0, 0), (0, LANES - KV_HEADS * 16)))
    weights = [w.astype(BF16) for w in (wq, _rot_cols(wq), wk, _rot_cols(wk), wv, wgp, ws)]

    half = HEAD_DIM // 2
    inv = jnp.power(ROPE_THETA, -jnp.arange(half, dtype=F32) * 2.0 / HEAD_DIM)
    ang = jnp.arange(l, dtype=F32)[:, None] * inv[None, :]
    cos = jnp.tile(jnp.cos(ang), (1, LANES // half))
    sin = jnp.tile(jnp.sin(ang), (1, LANES // half))

    n_q = l // Q_TILE
    sub = tm // Q_TILE
    row = lambda i: (i, 0)
    const = lambda i: (0, 0)
    out_shape = [
        jax.ShapeDtypeStruct((n_q, KV_HEADS, Q_PER_KV * HEAD_DIM, Q_TILE), F32),
        jax.ShapeDtypeStruct((l, KV_WIDTH), F32),
        jax.ShapeDtypeStruct((l, KV_WIDTH), BF16),
        jax.ShapeDtypeStruct((l, KV_WIDTH), BF16),
        jax.ShapeDtypeStruct((l, KV_WIDTH), F32),
        jax.ShapeDtypeStruct((n_q, KV_WIDTH, Q_TILE), BF16),
        jax.ShapeDtypeStruct((n_q, KV_WIDTH, Q_TILE), BF16),
        jax.ShapeDtypeStruct((n_q, KV_HEADS, 16, Q_TILE), F32),
        jax.ShapeDtypeStruct((l, ssm_w), F32),
    ]
    out_specs = [
        pl.BlockSpec((sub, KV_HEADS, Q_PER_KV * HEAD_DIM, Q_TILE), lambda i: (i, 0, 0, 0)),
        pl.BlockSpec((tm, KV_WIDTH), row),
        pl.BlockSpec((tm, KV_WIDTH), row),
        pl.BlockSpec((tm, KV_WIDTH), row),
        pl.BlockSpec((tm, KV_WIDTH), row),
        pl.BlockSpec((sub, KV_WIDTH, Q_TILE), lambda i: (i, 0, 0)),
        pl.BlockSpec((sub, KV_WIDTH, Q_TILE), lambda i: (i, 0, 0)),
        pl.BlockSpec((sub, KV_HEADS, 16, Q_TILE), lambda i: (i, 0, 0, 0)),
        pl.BlockSpec((tm, ssm_w), row),
    ]
    in_specs = [pl.BlockSpec((tm, d), row),
                pl.BlockSpec((1, d), const), pl.BlockSpec((1, d), const), pl.BlockSpec((1, d), const),
                pl.BlockSpec((tm, LANES), row), pl.BlockSpec((tm, LANES), row)]
    in_specs += [pl.BlockSpec(w.shape, const) for w in weights]
    return pl.pallas_call(
        _inproj_kernel,
        grid=(l // tm,),
        in_specs=in_specs,
        out_specs=out_specs,
        out_shape=out_shape,
        compiler_params=_params(1),
        name="inproj",
    )(x2, norm_g.reshape(1, d), sc, sh, cos, sin, *weights)


def _compress_kernel(x_ref, pe_ref, w1_ref, w2_ref, o_ref, *, transpose_out):
    x = x_ref[0]
    half = x.shape[1]
    w1 = w1_ref[...]
    first = _dot_hp(x, w1[0:half])
    second = _dot_hp(x, w1[half:2 * half])
    bias = _dot_hp(pe_ref[...], w1)[0:1]
    n = x.shape[0]
    h1 = first + pltpu.roll(second, n - 1, 0) + bias
    out = _dot_hp(_gelu_tanh(h1), w2_ref[...])
    if transpose_out:
        o_ref[0] = out.T[0:HEAD_DIM, :]
    else:
        o_ref[0] = out[:, 0:HEAD_DIM]


def _compress(z, pe, w1, w2, transpose_out):
    l = z.shape[0]
    n = l // CMP_STRIDE
    x16 = z.reshape(n, CMP_STRIDE, KV_HEADS, HEAD_DIM).transpose(2, 0, 1, 3)
    x16 = x16.reshape(KV_HEADS, n, CMP_STRIDE * HEAD_DIM)
    pe8 = jnp.broadcast_to(pe.reshape(1, CMP_BLOCK * HEAD_DIM), (8, CMP_BLOCK * HEAD_DIM))
    w2p = jnp.pad(w2, ((0, 0), (0, LANES - HEAD_DIM)))
    oshape = (KV_HEADS, HEAD_DIM, n) if transpose_out else (KV_HEADS, n, HEAD_DIM)
    return pl.pallas_call(
        functools.partial(_compress_kernel, transpose_out=transpose_out),
        grid=(KV_HEADS,),
        in_specs=[pl.BlockSpec((1, n, CMP_STRIDE * HEAD_DIM), lambda h: (h, 0, 0)),
                  pl.BlockSpec(pe8.shape, lambda h: (0, 0)),
                  pl.BlockSpec(w1.shape, lambda h: (0, 0)),
                  pl.BlockSpec(w2p.shape, lambda h: (0, 0))],
        out_specs=pl.BlockSpec((1,) + oshape[1:], lambda h: (h, 0, 0)),
        out_shape=jax.ShapeDtypeStruct(oshape, F32),
        compiler_params=_params(1),
        name="compress_v" if transpose_out else "compress_k",
    )(x16, pe8, w1, w2p)


def _tile4(a):
    return jnp.concatenate([a] * Q_PER_KV, axis=1)


MASKED = -1e30
P_PAD = 8


def _attn_kernel(qT_ref, kc_ref, vcT_ref, ks_ref, vsT_ref, *rest):
    kw_refs = rest[0:WIN_CHUNKS]
    vw_refs = rest[WIN_CHUNKS:2 * WIN_CHUNKS]
    gT_ref, o_ref, p_scr, bias_scr, s_scr, pb_scr = rest[2 * WIN_CHUNKS:]
    h = pl.program_id(0)
    i = pl.program_id(1)
    n_cmp = kc_ref.shape[1]
    n_sel = bias_scr.shape[0]
    G, DH, TQ = Q_PER_KV, HEAD_DIM, Q_TILE

    qT = qT_ref[0, 0]
    q_all = jnp.concatenate([qT[g * DH:(g + 1) * DH, :] for g in range(G)], axis=1)
    zeros = jnp.zeros_like(q_all)
    q_pad = jnp.where(h == 0, jnp.concatenate([q_all, zeros], axis=0),
                      jnp.concatenate([zeros, q_all], axis=0)).astype(BF16)
    head_rows = pl.ds(pl.multiple_of(h * DH, DH), DH)

    t = i * TQ + jax.lax.broadcasted_iota(jnp.int32, (1, TQ), 1)

    c_end = jax.lax.broadcasted_iota(jnp.int32, (n_cmp, 1), 0) * CMP_STRIDE + (CMP_BLOCK - 1)
    s = _dot_hp(kc_ref[0], q_all) + _tile4(jnp.where(c_end <= t, 0.0, MASKED))
    m = jnp.max(s, axis=0, keepdims=True)
    e = jnp.exp2(s - m)
    any_valid = jnp.where(m > 0.5 * MASKED, 1.0, 0.0)
    p = e * (any_valid / jnp.sum(e, axis=0, keepdims=True))
    o_c = _dot(vcT_ref[0].astype(BF16), p.astype(BF16))
    p_sum = p[:, 0:TQ]
    for g in range(1, G):
        p_sum = p_sum + p[:, g * TQ:(g + 1) * TQ]
    p_scr[0:P_PAD, :] = jnp.zeros((P_PAD, TQ), F32)
    p_scr[P_PAD:P_PAD + n_cmp, :] = p_sum

    ratio = SEL_BLOCK // CMP_STRIDE

    def every(off):
        return p_scr[pl.ds(P_PAD + off, n_sel, stride=ratio), :]

    imp = 0.5 * (every(-1) + every(3)) + (every(0) + every(1) + every(2))
    blk = jax.lax.broadcasted_iota(jnp.int32, (n_sel, 1), 0)
    blk_f = blk.astype(F32)
    cur = jax.lax.shift_right_logical(t, int(math.log2(SEL_BLOCK)))
    forced = (blk == 0) | (blk == cur) | (blk == cur - 1)
    score = jnp.where(blk > cur, NEG_INF, jnp.where(forced, FORCE_SCORE, imp))

    def pick_one(_, carry):
        score, bias = carry
        m = jnp.max(score, axis=0, keepdims=True)
        first = jnp.min(jnp.where(score == m, blk_f, float(n_sel)), axis=0, keepdims=True)
        hit = blk_f == first
        bias = jnp.where(hit & (m > NEG_INF), 0.0, bias)
        return jnp.where(hit, NEG_INF, score), bias

    _, bias = jax.lax.fori_loop(0, min(SEL_TOPN, n_sel), pick_one,
                                (score, jnp.full((n_sel, TQ), MASKED, F32)))

    KC = KEY_CHUNK
    BPC = KC // SEL_BLOCK
    VROWS = DH + 16
    own = (blk == 2 * i) | (blk == 2 * i + 1)
    own_lo = jnp.max(jnp.where(blk == 2 * i, bias, MASKED), axis=0, keepdims=True)
    own_hi = jnp.max(jnp.where(blk == 2 * i + 1, bias, MASKED), axis=0, keepdims=True)
    bias_scr[...] = jnp.where(own, MASKED, bias)

    onehot = (jax.lax.broadcasted_iota(jnp.int32, (KC, LANES), 1)
              == jax.lax.shift_right_logical(jax.lax.broadcasted_iota(jnp.int32, (KC, LANES), 0),
                                             int(math.log2(SEL_BLOCK)))).astype(BF16)
    ones_rows = jnp.ones((VROWS - DH, KC), BF16)
    last_chunk = ks_ref.shape[0] // KC - 1

    def scores(c):
        k = ks_ref[pl.ds(pl.multiple_of(c * KC, KC), KC), :]
        rows = bias_scr[pl.ds(pl.multiple_of(c * BPC, BPC), BPC), :]
        extra = jnp.concatenate([_tile4(rows), jnp.zeros((LANES - BPC, G * TQ), F32)], axis=0)
        s = _dot(jnp.concatenate([k, onehot], axis=1),
                 jnp.concatenate([q_pad, extra.astype(BF16)], axis=0))
        return s, jnp.max(s, axis=0, keepdims=True)

    def values(c):
        v = [vsT_ref[(KC // TQ) * c + j, head_rows, :] for j in range(KC // TQ)]
        return jnp.concatenate([jnp.concatenate(v, axis=1), ones_rows], axis=0)

    def pair(j, carry):
        m_run, acc, mx_a = carry
        ca = 2 * j
        acc = acc + _dot(values(jnp.maximum(ca - 1, 0)), pb_scr[...])
        s_b, mx_b = scores(ca + 1)
        s_scr[1] = s_b
        m_a = jnp.maximum(m_run, mx_a)
        p_a = jnp.exp2(s_scr[0] - m_a).astype(BF16)
        acc = jnp.exp2(m_run - m_a) * acc + _dot(values(ca), p_a)
        s_a, mx_next = scores(jnp.minimum(ca + 2, last_chunk))
        s_scr[0] = s_a
        m_b = jnp.maximum(m_a, mx_b)
        pb_scr[...] = jnp.exp2(s_scr[1] - m_b).astype(BF16)
        return m_b, jnp.exp2(m_a - m_b) * acc, mx_next

    n_pairs = ((i + KC // TQ - 1) // (KC // TQ) + 1) // 2
    s_0, mx_0 = scores(0)
    s_scr[0] = s_0
    pb_scr[...] = jnp.zeros(pb_scr.shape, BF16)
    init = (jnp.full((1, G * TQ), MASKED, F32), jnp.zeros((VROWS, G * TQ), F32), mx_0)
    m_run, acc, _ = jax.lax.fori_loop(0, n_pairs, pair, init)
    acc = acc + _dot(values(jnp.maximum(2 * n_pairs - 1, 0)), pb_scr[...])

    row_o = jax.lax.broadcasted_iota(jnp.int32, (TQ, 1), 0)
    lane_o = jax.lax.broadcasted_iota(jnp.int32, (1, TQ), 1)
    own_bias = jnp.where(row_o <= lane_o, jnp.where(row_o < SEL_BLOCK, own_lo, own_hi), MASKED)
    s = _dot(ks_ref[pl.ds(pl.multiple_of(i * TQ, TQ), TQ), :], q_pad) + _tile4(own_bias)
    m_s = jnp.maximum(m_run, jnp.max(s, axis=0, keepdims=True))
    p = jnp.exp2(s - m_s).astype(BF16)
    v_own = jnp.concatenate([vsT_ref[i, head_rows, :], ones_rows[:, 0:TQ]], axis=0)
    acc = jnp.exp2(m_run - m_s) * acc + _dot(v_own, p)
    o_s = acc[0:DH, :] / acc[DH:DH + 1, :]

    k_w = jnp.concatenate([r[...] for r in kw_refs], axis=0)
    v_wT = jnp.concatenate([r[0, head_rows, :] for r in vw_refs], axis=1)
    w_pos = (i - (WIN_CHUNKS - 1)) * TQ + jax.lax.broadcasted_iota(jnp.int32, (WIN_CHUNKS * TQ, 1), 0)
    in_window = (w_pos <= t) & (w_pos > t - WINDOW) & (w_pos >= 0)
    s = _dot(k_w, q_pad) + _tile4(jnp.where(in_window, 0.0, MASKED))
    p = jnp.exp2(s - jnp.max(s, axis=0, keepdims=True))
    o_w = _dot(v_wT, p.astype(BF16)) / jnp.sum(p, axis=0, keepdims=True)

    gates = gT_ref[0, 0]
    outs = []
    for g in range(G):
        cols = slice(g * TQ, (g + 1) * TQ)
        outs.append(gates[3 * g:3 * g + 1, :] * o_c[:, cols]
                    + gates[3 * g + 1:3 * g + 2, :] * o_s[:, cols]
                    + gates[3 * g + 2:3 * g + 3, :] * o_w[:, cols])
    o_ref[...] = jnp.concatenate(outs, axis=0).T


def _attention(qT, kc, vcT, ks, vsT, kw, vwT, gT, l):
    n_q = l // Q_TILE
    n_cmp = l // CMP_STRIDE
    n_sel = l // SEL_BLOCK
    assert CMP_BLOCK == 2 * CMP_STRIDE and SEL_BLOCK == 4 * CMP_STRIDE

    def win(j):
        return lambda h, i: (jnp.maximum(i - (WIN_CHUNKS - 1) + j, 0), 0)

    def win3(j):
        return lambda h, i: (jnp.maximum(i - (WIN_CHUNKS - 1) + j, 0), 0, 0)

    in_specs = [
        pl.BlockSpec((1, 1, Q_PER_KV * HEAD_DIM, Q_TILE), lambda h, i: (i, h, 0, 0)),
        pl.BlockSpec((1, n_cmp, HEAD_DIM), lambda h, i: (h, 0, 0)),
        pl.BlockSpec((1, HEAD_DIM, n_cmp), lambda h, i: (h, 0, 0)),
        pl.BlockSpec(ks.shape, lambda h, i: (0, 0)),
        pl.BlockSpec(vsT.shape, lambda h, i: (0, 0, 0)),
    ]
    in_specs += [pl.BlockSpec((Q_TILE, KV_WIDTH), win(j)) for j in range(WIN_CHUNKS)]
    in_specs += [pl.BlockSpec((1, KV_WIDTH, Q_TILE), win3(j)) for j in range(WIN_CHUNKS)]
    in_specs += [pl.BlockSpec((1, 1, 16, Q_TILE), lambda h, i: (i, h, 0, 0))]
    return pl.pallas_call(
        _attn_kernel,
        grid=(KV_HEADS, n_q),
        in_specs=in_specs,
        out_specs=pl.BlockSpec((Q_TILE, Q_PER_KV * HEAD_DIM), lambda h, i: (i, h)),
        out_shape=jax.ShapeDtypeStruct((l, ATTN_WIDTH), F32),
        scratch_shapes=[pltpu.VMEM((P_PAD + n_cmp, Q_TILE), F32),
                        pltpu.VMEM((n_sel, Q_TILE), F32),
                        pltpu.VMEM((2, KEY_CHUNK, Q_PER_KV * Q_TILE), F32),
                        pltpu.VMEM((KEY_CHUNK, Q_PER_KV * Q_TILE), BF16)],
        compiler_params=_params(2),
        name="attn",
    )(qT, kc, vcT, ks, vsT, *([kw] * WIN_CHUNKS), *([vwT] * WIN_CHUNKS), gT)


SSM_TILE = 128
SSM_PACK = 16


def _ssm_kernel(u_ref, a_ref, wb_ref, wc_ref, msk_ref, d_ref, wg_ref, bg_ref, y_ref,
                urow_scr, x_scr, yall_scr, y_scr, st_scr):
    step = pl.program_id(0)
    tc, width = u_ref.shape
    P = SSM_PACK

    @pl.when(step == 0)
    def _():
        st_scr[...] = jnp.zeros_like(st_scr)

    msk = msk_ref[...]

    def spread(ti, _):
        rows = pl.ds(pl.multiple_of(ti * P, P), P)
        urow_scr[rows, :] = (jnp.broadcast_to(u_ref[pl.ds(ti, 1), :], (P, width)) * msk).astype(BF16)
        return 0

    jax.lax.fori_loop(0, tc, spread, 0, unroll=8)
    x_scr[...] = _dot(urow_scr[...], wb_ref[...])

    a_re = a_ref[0]
    a_im = a_ref[1]

    def one(ti, carry):
        x_re, x_im = carry
        rows = pl.ds(pl.multiple_of(ti * P, P), P)
        n_re = a_re * x_re - a_im * x_im + x_scr[rows, 0:LANES]
        n_im = a_re * x_im + a_im * x_re + x_scr[rows, LANES:2 * LANES]
        x_scr[rows, 0:LANES] = n_re
        x_scr[rows, LANES:2 * LANES] = n_im
        return n_re, n_im

    x_re, x_im = jax.lax.fori_loop(0, tc, one, (st_scr[0], st_scr[1]), unroll=8)
    st_scr[0] = x_re
    st_scr[1] = x_im

    yall_scr[...] = _dot(x_scr[...].astype(BF16), wc_ref[...])

    def gather(ti, _):
        rows = pl.ds(pl.multiple_of(ti * P, P), P)
        y_scr[pl.ds(ti, 1), :] = jnp.sum(yall_scr[rows, :] * msk, axis=0, keepdims=True)
        return 0

    jax.lax.fori_loop(0, tc, gather, 0, unroll=8)
    u = u_ref[...]
    y = _gelu_tanh(y_scr[...] + d_ref[...] * u)
    y_ref[...] = y * _sigmoid(_dot(y.astype(BF16), wg_ref[...]) + bg_ref[...])


def _s5(zs, lam_re, lam_im, log_step, b_re, b_im, c_re, c_im, d_skip, w_glu, b_glu):
    l, width = zs.shape
    G, Pn, C = b_re.shape
    assert Pn == SSM_STATE and C == SSM_GROUP and G * C == width and G == 2 * SSM_PACK
    step = jnp.exp(log_step)[:, None]
    mag = jnp.exp(lam_re * step)
    a_re = mag * jnp.cos(lam_im * step)
    a_im = mag * jnp.sin(lam_im * step)
    den = lam_re * lam_re + lam_im * lam_im
    n_re = a_re - 1.0
    f_re = (n_re * lam_re + a_im * lam_im) / den
    f_im = (a_im * lam_re - n_re * lam_im) / den
    bb_re = f_re[..., None] * b_re - f_im[..., None] * b_im
    bb_im = f_re[..., None] * b_im + f_im[..., None] * b_re
    a_pack = jnp.stack([a_re.reshape(SSM_PACK, 2 * Pn), a_im.reshape(SSM_PACK, 2 * Pn)])

    eye2 = jnp.eye(2, dtype=F32)

    def b_stack(bb):
        t = bb.reshape(SSM_PACK, 2, Pn, C)
        return jnp.einsum('jgpc,gh->jhcgp', t, eye2).reshape(width, 2 * Pn)

    w_b = jnp.concatenate([b_stack(bb_re), b_stack(bb_im)], axis=1).astype(BF16)

    def c_stack(cc):
        t = cc.reshape(SSM_PACK, 2, C, Pn)
        return jnp.einsum('jgcp,gh->gpjhc', t, eye2).reshape(2 * Pn, width)

    w_c = jnp.concatenate([c_stack(c_re), -c_stack(c_im)], axis=0).astype(BF16)
    lane_row = jnp.arange(width) // (2 * C)
    msk = (lane_row[None, :] == jnp.arange(SSM_PACK)[:, None]).astype(F32)

    tc = SSM_TILE
    const2 = lambda i: (0, 0)
    return pl.pallas_call(
        _ssm_kernel,
        grid=(l // tc,),
        in_specs=[pl.BlockSpec((tc, width), lambda i: (i, 0)),
                  pl.BlockSpec(a_pack.shape, lambda i: (0, 0, 0)),
                  pl.BlockSpec(w_b.shape, const2),
                  pl.BlockSpec(w_c.shape, const2),
                  pl.BlockSpec(msk.shape, const2),
                  pl.BlockSpec((1, width), const2),
                  pl.BlockSpec((width, width), const2),
                  pl.BlockSpec((1, width), const2)],
        out_specs=pl.BlockSpec((tc, width), lambda i: (i, 0)),
        out_shape=jax.ShapeDtypeStruct((l, width), F32),
        scratch_shapes=[pltpu.VMEM((tc * SSM_PACK, width), BF16),
                        pltpu.VMEM((tc * SSM_PACK, 2 * LANES), F32),
                        pltpu.VMEM((tc * SSM_PACK, width), F32),
                        pltpu.VMEM((tc, width), F32),
                        pltpu.VMEM((2, SSM_PACK, LANES), F32)],
        compiler_params=_params(1),
        name="ssm",
    )(zs, a_pack, w_b, w_c, msk, d_skip.reshape(1, width), w_glu.astype(BF16), b_glu.reshape(1, width))


def _outproj_kernel(x_ref, oa_ref, ys_ref, ga_ref, gs_ref, wa_ref, ws_ref, gt_ref, o_ref):
    a = (_rms_scale(oa_ref[...]) * ga_ref[...]).astype(BF16)
    s = (_rms_scale(ys_ref[...]) * gs_ref[...]).astype(BF16)
    y = _dot(a, wa_ref[...]) + _dot(s, ws_ref[...])
    o_ref[...] = x_ref[...] + gt_ref[...] * y


def _out_projection(x2, o_attn, y_ssm, attn_g, ssm_g, w_o, gt):
    l, d = x2.shape
    tm = ROW_TILE
    aw, sw = o_attn.shape[1], y_ssm.shape[1]
    row = lambda i: (i, 0)
    const = lambda i: (0, 0)
    return pl.pallas_call(
        _outproj_kernel,
        grid=(l // tm,),
        in_specs=[pl.BlockSpec((tm, d), row), pl.BlockSpec((tm, aw), row), pl.BlockSpec((tm, sw), row),
                  pl.BlockSpec((1, aw), const), pl.BlockSpec((1, sw), const),
                  pl.BlockSpec((aw, d), const), pl.BlockSpec((sw, d), const), pl.BlockSpec((1, d), const)],
        out_specs=pl.BlockSpec((tm, d), row),
        out_shape=jax.ShapeDtypeStruct((l, d), F32),
        compiler_params=_params(1),
        name="outproj",
    )(x2, o_attn, y_ssm, attn_g.reshape(1, aw), ssm_g.reshape(1, sw),
      w_o[:aw].astype(BF16), w_o[aw:].astype(BF16), gt)


FFN_TILE = 256
FFN_COLS = 256
HALO = 8


def _ffn_kernel(x_ref, g_ref, sc_ref, sh_ref, wup_ref, cw_ref, cb_ref, wdn_ref, gt_ref, gf_ref,
                o_ref, ua_scr, uv_scr):
    step = pl.program_id(0)
    tm = x_ref.shape[0]
    n_col = wdn_ref.shape[0]

    @pl.when(step == 0)
    def _():
        ua_scr[...] = jnp.zeros_like(ua_scr)
        uv_scr[...] = jnp.zeros_like(uv_scr)

    x = x_ref[...]
    hb = (_rms_scale(x) * g_ref[...] * (1.0 + sc_ref[...]) + sh_ref[...]).astype(BF16)

    def conv(scr, j, u, col):
        scr[j, 0:HALO, :] = scr[j, tm:tm + HALO, :]
        scr[j, HALO:tm + HALO, :] = u
        w = cw_ref[col]
        out = cb_ref[col]
        for k in range(CONV_WIDTH):
            off = HALO - (CONV_WIDTH - 1) + k
            out = out + w[k:k + 1, :] * scr[j, off:off + tm, :]
        return out

    def col_step(j, acc):
        a = conv(ua_scr, j, _dot(hb, wup_ref[j]), j)
        v = conv(uv_scr, j, _dot(hb, wup_ref[n_col + j]), n_col + j)
        act = (a * _sigmoid(a) * v).astype(BF16)
        return acc + _dot(act, wdn_ref[j])

    y = jax.lax.fori_loop(0, n_col, col_step, jnp.zeros(x.shape, F32))
    x2 = x + gt_ref[...] * y
    o_ref[...] = _rms_scale(x2) * gf_ref[...]


def _conv_ffn(x1, norm_g, sc, sh, w_up, conv_w, conv_b, w_down, gt, normf_g):
    l, d = x1.shape
    f = w_down.shape[0]
    tm, cw = FFN_TILE, FFN_COLS
    n_col = f // cw
    wup = w_up.astype(BF16).reshape(d, 2 * n_col, cw).transpose(1, 0, 2)
    cwt = conv_w.reshape(CONV_WIDTH, 2 * n_col, cw).transpose(1, 0, 2)
    cbt = conv_b.reshape(2 * n_col, 1, cw)
    wdn = w_down.astype(BF16).reshape(n_col, cw, d)
    row = lambda i: (i, 0)
    const = lambda i: (0, 0)
    const3 = lambda i: (0, 0, 0)
    return pl.pallas_call(
        _ffn_kernel,
        grid=(l // tm,),
        in_specs=[pl.BlockSpec((tm, d), row),
                  pl.BlockSpec((1, d), const), pl.BlockSpec((1, d), const), pl.BlockSpec((1, d), const),
                  pl.BlockSpec(wup.shape, const3), pl.BlockSpec(cwt.shape, const3),
                  pl.BlockSpec(cbt.shape, const3), pl.BlockSpec(wdn.shape, const3),
                  pl.BlockSpec((1, d), const), pl.BlockSpec((1, d), const)],
        out_specs=pl.BlockSpec((tm, d), row),
        out_shape=jax.ShapeDtypeStruct((l, d), F32),
        scratch_shapes=[pltpu.VMEM((n_col, tm + HALO, cw), F32),
                        pltpu.VMEM((n_col, tm + HALO, cw), F32)],
        compiler_params=_params(1),
        name="ffn",
    )(x1, norm_g.reshape(1, d), sc, sh, wup, cwt, cbt, wdn, gt, normf_g.reshape(1, d))


def kernel(x, c, norm1_g, norm2_g, normf_g, w_ada, b_ada, w_in, pe_k, pe_v, w_ck1, w_ck2, w_cv1, w_cv2,
           lam_re, lam_im, log_step, b_re, b_im, c_re, c_im, d_skip, w_glu, b_glu, attn_norm_g,
           ssm_norm_g, w_o, w_up, conv_w, conv_b, w_down):
    b, l, d = x.shape
    depth = w_ada.shape[0]
    assert b == 1 and depth == 1
    assert l % KEY_CHUNK == 0 and l % ROW_TILE == 0 and l >= WINDOW
    x2 = x.reshape(l, d)
    mod = _modulation(c, w_ada[0], b_ada[0])
    sh1, sc1, gt1, sh2, sc2, gt2 = [mod[:, k * d:(k + 1) * d] for k in range(6)]

    qT, k_c, k_s, k_w, v_c, v_sT, v_wT, gT, zs = _in_projection(x2, norm1_g[0], sc1, sh1, w_in[0])
    kc = _compress(k_c, pe_k[0], w_ck1[0], w_ck2[0], transpose_out=False)
    vcT = _compress(v_c, pe_v[0], w_cv1[0], w_cv2[0], transpose_out=True)
    o_attn = _attention(qT, kc, vcT, k_s, v_sT, k_w, v_wT, gT, l)
    y_ssm = _s5(zs, lam_re[0], lam_im[0], log_step[0], b_re[0], b_im[0], c_re[0], c_im[0],
                d_skip[0], w_glu[0], b_glu[0])
    x1 = _out_projection(x2, o_attn, y_ssm, attn_norm_g[0], ssm_norm_g[0], w_o[0], gt1)
    out = _conv_ffn(x1, norm2_g[0], sc2, sh2, w_up[0], conv_w[0], conv_b[0], w_down[0], gt2, normf_g)
    return out.reshape(b, l, d)
```

```python
import functools
import math

import jax
import jax.numpy as jnp
import numpy as np
from jax.experimental import pallas as pl
from jax.experimental.pallas import tpu as pltpu

ATTN_HEADS = 8
KV_HEADS = 2
HEAD_DIM = 64
Q_PER_KV = ATTN_HEADS // KV_HEADS
ATTN_WIDTH = ATTN_HEADS * HEAD_DIM
KV_WIDTH = KV_HEADS * HEAD_DIM
SSM_GROUP = 16
SSM_STATE = 64
CMP_BLOCK = 32
CMP_STRIDE = 16
SEL_BLOCK = 64
SEL_TOPN = 16
WINDOW = 512
FORCE_SCORE = 1.0e4
ROPE_THETA = 10000.0
CONV_WIDTH = 3
NORM_EPS = 1e-6

LANES = 128
Q_TILE = 128
KEY_CHUNK = 512
ROW_TILE = 512
WIN_CHUNKS = WINDOW // Q_TILE + 1
VMEM_LIMIT = 56 * 1024 * 1024

F32 = jnp.float32
BF16 = jnp.bfloat16
NEG_INF = float("-inf")


def _params(n_axes, vmem=VMEM_LIMIT):
    return pltpu.CompilerParams(dimension_semantics=("arbitrary",) * n_axes,
                                vmem_limit_bytes=vmem)


def _split(a):
    hi = a.astype(BF16)
    lo = (a - hi.astype(F32)).astype(BF16)
    return hi, lo


def _dot(a, b):
    return jnp.dot(a, b, preferred_element_type=F32)


def _dot_hp(a, b):
    a_hi, a_lo = _split(a)
    b_hi, b_lo = _split(b)
    return _dot(a_hi, b_hi) + (_dot(a_hi, b_lo) + _dot(a_lo, b_hi))


def _sigmoid(x):
    return 1.0 / (1.0 + jnp.exp(-x))


def _gelu_tanh(x):
    c = math.sqrt(2.0 / math.pi)
    return 0.5 * x * (1.0 + jnp.tanh(c * (x + 0.044715 * (x * x * x))))


def _rms_scale(x):
    return x * jax.lax.rsqrt(jnp.mean(x * x, axis=-1, keepdims=True) + NORM_EPS)


def _mod_kernel(c_ref, w_ref, b_ref, o_ref):
    c = c_ref[...]
    s = c * _sigmoid(c)
    o_ref[...] = _dot_hp(s, w_ref[...]) + b_ref[...]


def _modulation(c, w_ada, b_ada):
    d, n = w_ada.shape
    bn = n // 4
    c8 = jnp.broadcast_to(c, (8, d))
    out = pl.pallas_call(
        _mod_kernel,
        grid=(n // bn,),
        in_specs=[pl.BlockSpec((8, d), lambda j: (0, 0)),
                  pl.BlockSpec((d, bn), lambda j: (0, j)),
                  pl.BlockSpec((1, bn), lambda j: (0, j))],
        out_specs=pl.BlockSpec((8, bn), lambda j: (0, j)),
        out_shape=jax.ShapeDtypeStruct((8, n), F32),
        compiler_params=_params(1),
        name="mod",
    )(c8, w_ada, b_ada.reshape(1, n))
    return out[0:1]


def _inproj_kernel(x_ref, g_ref, sc_ref, sh_ref, cos_ref, sin_ref,
                   wq_ref, wqr_ref, wk_ref, wkr_ref, wv_ref, wg_ref, ws_ref,
                   qT_ref, kc_ref, ks_ref, kw_ref, vc_ref, vsT_ref, vwT_ref, gT_ref, zs_ref):
    x = x_ref[...]
    h = _rms_scale(x) * g_ref[...] * (1.0 + sc_ref[...]) + sh_ref[...]
    hb = h.astype(BF16)
    cos = cos_ref[...]
    sin = sin_ref[...]
    n_sub = x.shape[0] // Q_TILE

    zq = _dot(hb, wq_ref[...])
    zqr = _dot(hb, wqr_ref[...])
    scale = HEAD_DIM ** -0.5 * math.log2(math.e)
    q_cols = []
    for j in range(ATTN_WIDTH // LANES):
        sl = slice(j * LANES, (j + 1) * LANES)
        q_cols.append((zq[:, sl] * cos + zqr[:, sl] * sin) * scale)
    q = jnp.concatenate(q_cols, axis=1)
    for s in range(n_sub):
        qs = q[s * Q_TILE:(s + 1) * Q_TILE, :].T
        qT_ref[s] = qs.reshape(KV_HEADS, Q_PER_KV * HEAD_DIM, Q_TILE)

    zk = _dot(hb, wk_ref[...])
    zkr = _dot(hb, wkr_ref[...])
    k_outs = (kc_ref, ks_ref, kw_ref)
    for j in range(3):
        sl = slice(j * LANES, (j + 1) * LANES)
        kj = zk[:, sl] * cos + zkr[:, sl] * sin
        k_outs[j][...] = kj.astype(k_outs[j].dtype)

    zv = _dot(hb, wv_ref[...])
    vc_ref[...] = zv[:, 0:LANES]
    vs = zv[:, LANES:2 * LANES]
    vw = zv[:, 2 * LANES:3 * LANES]
    for s in range(n_sub):
        vsT_ref[s] = vs[s * Q_TILE:(s + 1) * Q_TILE, :].T.astype(BF16)
        vwT_ref[s] = vw[s * Q_TILE:(s + 1) * Q_TILE, :].T.astype(BF16)

    gates = _sigmoid(_dot(hb, wg_ref[...]))
    for s in range(n_sub):
        gt = gates[s * Q_TILE:(s + 1) * Q_TILE, :].T
        gT_ref[s] = gt[0:2 * 16, :].reshape(KV_HEADS, 16, Q_TILE)

    zs_ref[...] = _dot(hb, ws_ref[...])


def _rot_cols(w):
    d, n = w.shape
    w4 = w.reshape(d, n // HEAD_DIM, 2, HEAD_DIM // 2)
    return jnp.concatenate([-w4[:, :, 1:2], w4[:, :, 0:1]], axis=2).reshape(d, n)


def _in_projection(x2, norm_g, sc, sh, w_in):
    l, d = x2.shape
    tm = ROW_TILE
    cuts = np.cumsum([ATTN_WIDTH] + [KV_WIDTH] * 6 + [3 * ATTN_HEADS])
    wq = w_in[:, :cuts[0]]
    wkc, wvc = w_in[:, cuts[0]:cuts[1]], w_in[:, cuts[1]:cuts[2]]
    wks, wvs = w_in[:, cuts[2]:cuts[3]], w_in[:, cuts[3]:cuts[4]]
    wkw, wvw = w_in[:, cuts[4]:cuts[5]], w_in[:, cuts[5]:cuts[6]]
    wg = w_in[:, cuts[6]:cuts[7]]
    ws = w_in[:, cuts[7]:]
    ssm_w = ws.shape[1]
    wk = jnp.concatenate([wkc, wks, wkw], axis=1)
    wv = jnp.concatenate([wvc, wvs, wvw], axis=1)
    wg3 = wg.reshape(d, KV_HEADS, Q_PER_KV * 3)
    wg3 = jnp.pad(wg3, ((0, 0), (0, 0), (0, 16 - Q_PER_KV * 3))).reshape(d, KV_HEADS * 16)
    wgp = jnp.pad(wg3, ((0, 0), (0, LANES - KV_HEADS * 16)))
    weights = [w.astype(BF16) for w in (wq, _rot_cols(wq), wk, _rot_cols(wk), wv, wgp, ws)]

    half = HEAD_DIM // 2
    inv = jnp.power(ROPE_THETA, -jnp.arange(half, dtype=F32) * 2.0 / HEAD_DIM)
    ang = jnp.arange(l, dtype=F32)[:, None] * inv[None, :]
    cos = jnp.tile(jnp.cos(ang), (1, LANES // half))
    sin = jnp.tile(jnp.sin(ang), (1, LANES // half))

    n_q = l // Q_TILE
    sub = tm // Q_TILE
    row = lambda i: (i, 0)
    const = lambda i: (0, 0)
    out_shape = [
        jax.ShapeDtypeStruct((n_q, KV_HEADS, Q_PER_KV * HEAD_DIM, Q_TILE), F32),
        jax.ShapeDtypeStruct((l, KV_WIDTH), F32),
        jax.ShapeDtypeStruct((l, KV_WIDTH), BF16),
        jax.ShapeDtypeStruct((l, KV_WIDTH), BF16),
        jax.ShapeDtypeStruct((l, KV_WIDTH), F32),
        jax.ShapeDtypeStruct((n_q, KV_WIDTH, Q_TILE), BF16),
        jax.ShapeDtypeStruct((n_q, KV_WIDTH, Q_TILE), BF16),
        jax.ShapeDtypeStruct((n_q, KV_HEADS, 16, Q_TILE), F32),
        jax.ShapeDtypeStruct((l, ssm_w), F32),
    ]
    out_specs = [
        pl.BlockSpec((sub, KV_HEADS, Q_PER_KV * HEAD_DIM, Q_TILE), lambda i: (i, 0, 0, 0)),
        pl.BlockSpec((tm, KV_WIDTH), row),
        pl.BlockSpec((tm, KV_WIDTH), row),
        pl.BlockSpec((tm, KV_WIDTH), row),
        pl.BlockSpec((tm, KV_WIDTH), row),
        pl.BlockSpec((sub, KV_WIDTH, Q_TILE), lambda i: (i, 0, 0)),
        pl.BlockSpec((sub, KV_WIDTH, Q_TILE), lambda i: (i, 0, 0)),
        pl.BlockSpec((sub, KV_HEADS, 16, Q_TILE), lambda i: (i, 0, 0, 0)),
        pl.BlockSpec((tm, ssm_w), row),
    ]
    in_specs = [pl.BlockSpec((tm, d), row),
                pl.BlockSpec((1, d), const), pl.BlockSpec((1, d), const), pl.BlockSpec((1, d), const),
                pl.BlockSpec((tm, LANES), row), pl.BlockSpec((tm, LANES), row)]
    in_specs += [pl.BlockSpec(w.shape, const) for w in weights]
    return pl.pallas_call(
        _inproj_kernel,
        grid=(l // tm,),
        in_specs=in_specs,
        out_specs=out_specs,
        out_shape=out_shape,
        compiler_params=_params(1),
        name="inproj",
    )(x2, norm_g.reshape(1, d), sc, sh, cos, sin, *weights)


def _compress_kernel(x_ref, pe_ref, w1_ref, w2_ref, o_ref, *, transpose_out):
    x = x_ref[0]
    half = x.shape[1]
    w1 = w1_ref[...]
    first = _dot_hp(x, w1[0:half])
    second = _dot_hp(x, w1[half:2 * half])
    bias = _dot_hp(pe_ref[...], w1)[0:1]
    n = x.shape[0]
    h1 = first + pltpu.roll(second, n - 1, 0) + bias
    out = _dot_hp(_gelu_tanh(h1), w2_ref[...])
    if transpose_out:
        o_ref[0] = out.T[0:HEAD_DIM, :]
    else:
        o_ref[0] = out[:, 0:HEAD_DIM]


def _compress(z, pe, w1, w2, transpose_out):
    l = z.shape[0]
    n = l // CMP_STRIDE
    x16 = z.reshape(n, CMP_STRIDE, KV_HEADS, HEAD_DIM).transpose(2, 0, 1, 3)
    x16 = x16.reshape(KV_HEADS, n, CMP_STRIDE * HEAD_DIM)
    pe8 = jnp.broadcast_to(pe.reshape(1, CMP_BLOCK * HEAD_DIM), (8, CMP_BLOCK * HEAD_DIM))
    w2p = jnp.pad(w2, ((0, 0), (0, LANES - HEAD_DIM)))
    oshape = (KV_HEADS, HEAD_DIM, n) if transpose_out else (KV_HEADS, n, HEAD_DIM)
    return pl.pallas_call(
        functools.partial(_compress_kernel, transpose_out=transpose_out),
        grid=(KV_HEADS,),
        in_specs=[pl.BlockSpec((1, n, CMP_STRIDE * HEAD_DIM), lambda h: (h, 0, 0)),
                  pl.BlockSpec(pe8.shape, lambda h: (0, 0)),
                  pl.BlockSpec(w1.shape, lambda h: (0, 0)),
                  pl.BlockSpec(w2p.shape, lambda h: (0, 0))],
        out_specs=pl.BlockSpec((1,) + oshape[1:], lambda h: (h, 0, 0)),
        out_shape=jax.ShapeDtypeStruct(oshape, F32),
        compiler_params=_params(1),
        name="compress_v" if transpose_out else "compress_k",
    )(x16, pe8, w1, w2p)


def _tile4(a):
    return jnp.concatenate([a] * Q_PER_KV, axis=1)


MASKED = -1e30
P_PAD = 8


SEL_STEP = 64
N_FORCED = 3


def _attn_kernel(qT_ref, kc_ref, vcT_ref, ks_ref, vsT_ref, *rest):
    kw_refs = rest[0:WIN_CHUNKS]
    vw_refs = rest[WIN_CHUNKS:2 * WIN_CHUNKS]
    gT_ref, o_ref, p_scr, bias_scr, own_scr, oc_scr = rest[2 * WIN_CHUNKS:2 * WIN_CHUNKS + 6]
    per_head = rest[2 * WIN_CHUNKS + 6:]
    s_even, s_odd, p_even, p_odd = per_head[0::4], per_head[1::4], per_head[2::4], per_head[3::4]
    i = pl.program_id(0)
    n_sel = bias_scr.shape[0]
    H, G, DH, TQ = KV_HEADS, Q_PER_KV, HEAD_DIM, Q_TILE
    W = G * TQ
    heads = range(H)

    q_all, q_pad = [], []
    for h in heads:
        qT = qT_ref[0, h]
        qa = jnp.concatenate([qT[g * DH:(g + 1) * DH, :] for g in range(G)], axis=1)
        zeros = jnp.zeros_like(qa)
        q_all.append(qa)
        q_pad.append(jnp.concatenate([qa if hh == h else zeros for hh in heads], axis=0).astype(BF16))
    v_rows = [slice(h * DH, (h + 1) * DH) for h in heads]

    t = i * TQ + jax.lax.broadcasted_iota(jnp.int32, (1, TQ), 1)
    cur = jax.lax.shift_right_logical(t, int(math.log2(SEL_BLOCK)))

    ratio = SEL_BLOCK // CMP_STRIDE

    def select(ns):
        nc = ns * ratio
        c_end = jax.lax.broadcasted_iota(jnp.int32, (nc, 1), 0) * CMP_STRIDE + (CMP_BLOCK - 1)
        c_bias = _tile4(jnp.where(c_end <= t, 0.0, MASKED))
        blk = jax.lax.broadcasted_iota(jnp.int32, (ns, 1), 0)
        blk_f = blk.astype(F32)
        forced = (blk == 0) | (blk == cur) | (blk == cur - 1)
        scores = []
        for h in heads:
            s = _dot_hp(kc_ref[h, 0:nc, :], q_all[h]) + c_bias
            m = jnp.max(s, axis=0, keepdims=True)
            e = jnp.exp2(s - m)
            any_valid = jnp.where(m > 0.5 * MASKED, 1.0, 0.0)
            p = e * (any_valid / jnp.sum(e, axis=0, keepdims=True))
            oc_scr[h] = _dot(vcT_ref[h, :, 0:nc].astype(BF16), p.astype(BF16))
            p_sum = p[:, 0:TQ]
            for g in range(1, G):
                p_sum = p_sum + p[:, g * TQ:(g + 1) * TQ]
            p_scr[h, 0:P_PAD, :] = jnp.zeros((P_PAD, TQ), F32)
            p_scr[h, P_PAD:P_PAD + nc, :] = p_sum

            def every(off):
                return p_scr[h, pl.ds(P_PAD + off, ns, stride=ratio), :]

            imp = 0.5 * (every(-1) + every(3)) + (every(0) + every(1) + every(2))
            scores.append(jnp.where((blk > cur) | forced, NEG_INF, imp))
        score = jnp.concatenate(scores, axis=1)

        def pick_one(_, score):
            m = jnp.max(score, axis=0, keepdims=True)
            first = jnp.min(jnp.where(score == m, blk_f, float(ns)), axis=0, keepdims=True)
            return jnp.where(blk_f == first, NEG_INF, score)

        left = jax.lax.fori_loop(0, SEL_TOPN - N_FORCED, pick_one, score)
        cur_w = jnp.concatenate([cur] * H, axis=1)
        bias = jnp.where((left == NEG_INF) & (blk <= cur_w), 0.0, MASKED)
        own_scr[0:1, :] = jnp.max(jnp.where(blk == 2 * i, bias, MASKED), axis=0, keepdims=True)
        own_scr[1:2, :] = jnp.max(jnp.where(blk == 2 * i + 1, bias, MASKED), axis=0, keepdims=True)
        bias_scr[0:ns, :] = jnp.where((blk == 2 * i) | (blk == 2 * i + 1), MASKED, bias)
        if ns < n_sel:
            bias_scr[ns:n_sel, :] = jnp.full((n_sel - ns, H * TQ), MASKED, F32)

    sizes = [min(n_sel, SEL_STEP * (k + 1)) for k in range(pl.cdiv(n_sel, SEL_STEP))]
    variant = jnp.minimum((2 * i + 1) // SEL_STEP, len(sizes) - 1)
    for k, ns in enumerate(sizes):
        pl.when(variant == k)(functools.partial(select, ns))

    KC = KEY_CHUNK
    BPC = KC // SEL_BLOCK
    VROWS = DH + 16
    onehot = (jax.lax.broadcasted_iota(jnp.int32, (KC, LANES), 1)
              == jax.lax.shift_right_logical(jax.lax.broadcasted_iota(jnp.int32, (KC, LANES), 0),
                                             int(math.log2(SEL_BLOCK)))).astype(BF16)
    ones_rows = jnp.ones((VROWS - DH, KC), BF16)
    pad_rows = jnp.zeros((LANES - BPC, W), F32)
    last_chunk = ks_ref.shape[0] // KC - 1

    def scores(c):
        k = ks_ref[pl.ds(pl.multiple_of(c * KC, KC), KC), :]
        k_aug = jnp.concatenate([k, onehot], axis=1)
        rows = bias_scr[pl.ds(pl.multiple_of(c * BPC, BPC), BPC), :]
        out = []
        for h in heads:
            extra = jnp.concatenate([_tile4(rows[:, h * TQ:(h + 1) * TQ]), pad_rows], axis=0)
            s = _dot(k_aug, jnp.concatenate([q_pad[h], extra.astype(BF16)], axis=0))
            out.append((s, jnp.max(s, axis=0, keepdims=True)))
        return out

    def values(c):
        v = jnp.concatenate([vsT_ref[(KC // TQ) * c + j] for j in range(KC // TQ)], axis=1)
        return [jnp.concatenate([v[v_rows[h], :], ones_rows], axis=0) for h in heads]

    def chunk_step(c, m_run, acc, mx_cur, s_cur, s_next, p_prev, p_cur):
        v_prev = values(jnp.maximum(c - 1, 0))
        nxt = scores(jnp.minimum(c + 1, last_chunk))
        m_new, mx_next = [], []
        for h in heads:
            acc[h] = acc[h] + _dot(v_prev[h], p_prev[h][...])
            m_new.append(jnp.maximum(m_run[h], mx_cur[h]))
            acc[h] = jnp.exp2(m_run[h] - m_new[h]) * acc[h]
            p_cur[h][...] = jnp.exp2(s_cur[h][...] - m_new[h]).astype(BF16)
            s_next[h][...] = nxt[h][0]
            mx_next.append(nxt[h][1])
        return m_new, acc, mx_next

    def pair(j, carry):
        m_run, acc, mx = (list(c) for c in carry)
        m_run, acc, mx = chunk_step(2 * j, m_run, acc, mx, s_even, s_odd, p_odd, p_even)
        m_run, acc, mx = chunk_step(2 * j + 1, m_run, acc, mx, s_odd, s_even, p_even, p_odd)
        return tuple(m_run), tuple(acc), tuple(mx)

    n_pairs = ((i + KC // TQ - 1) // (KC // TQ) + 1) // 2
    s_0 = scores(0)
    for h in heads:
        s_even[h][...] = s_0[h][0]
        p_odd[h][...] = jnp.zeros(p_odd[h].shape, BF16)
    init = (tuple(jnp.full((1, W), MASKED, F32) for _ in heads),
            tuple(jnp.zeros((VROWS, W), F32) for _ in heads),
            tuple(s_0[h][1] for h in heads))
    m_run, acc, _ = jax.lax.fori_loop(0, n_pairs, pair, init)
    v_last = values(jnp.maximum(2 * n_pairs - 1, 0))
    pb_scr = p_odd

    row_o = jax.lax.broadcasted_iota(jnp.int32, (TQ, 1), 0)
    lane_o = jax.lax.broadcasted_iota(jnp.int32, (1, TQ), 1)
    k_own = ks_ref[pl.ds(pl.multiple_of(i * TQ, TQ), TQ), :]
    v_own = vsT_ref[i]
    k_w = jnp.concatenate([r[...] for r in kw_refs], axis=0)
    v_w = jnp.concatenate([r[0] for r in vw_refs], axis=1)
    w_pos = (i - (WIN_CHUNKS - 1)) * TQ + jax.lax.broadcasted_iota(jnp.int32, (WIN_CHUNKS * TQ, 1), 0)
    w_bias = _tile4(jnp.where((w_pos <= t) & (w_pos > t - WINDOW) & (w_pos >= 0), 0.0, MASKED))

    head_out = []
    for h in heads:
        a = acc[h] + _dot(v_last[h], pb_scr[h][...])
        own_lo = own_scr[0:1, h * TQ:(h + 1) * TQ]
        own_hi = own_scr[1:2, h * TQ:(h + 1) * TQ]
        own_bias = jnp.where(row_o <= lane_o, jnp.where(row_o < SEL_BLOCK, own_lo, own_hi), MASKED)
        s = _dot(k_own, q_pad[h]) + _tile4(own_bias)
        m_s = jnp.maximum(m_run[h], jnp.max(s, axis=0, keepdims=True))
        p = jnp.exp2(s - m_s).astype(BF16)
        a = jnp.exp2(m_run[h] - m_s) * a + _dot(jnp.concatenate([v_own[v_rows[h], :], ones_rows[:, 0:TQ]], axis=0), p)
        o_s = a[0:DH, :] / a[DH:DH + 1, :]

        s = _dot(k_w, q_pad[h]) + w_bias
        p = jnp.exp2(s - jnp.max(s, axis=0, keepdims=True))
        o_w = _dot(v_w[v_rows[h], :], p.astype(BF16)) / jnp.sum(p, axis=0, keepdims=True)

        o_c = oc_scr[h]
        gates = gT_ref[0, h]
        for g in range(G):
            cols = slice(g * TQ, (g + 1) * TQ)
            head_out.append(gates[3 * g:3 * g + 1, :] * o_c[:, cols]
                            + gates[3 * g + 1:3 * g + 2, :] * o_s[:, cols]
                            + gates[3 * g + 2:3 * g + 3, :] * o_w[:, cols])
    o_ref[...] = jnp.concatenate(head_out, axis=0).T


def _attention(qT, kc, vcT, ks, vsT, kw, vwT, gT, l):
    n_q = l // Q_TILE
    n_cmp = l // CMP_STRIDE
    n_sel = l // SEL_BLOCK
    assert CMP_BLOCK == 2 * CMP_STRIDE and SEL_BLOCK == 4 * CMP_STRIDE

    assert n_sel >= SEL_TOPN and KV_WIDTH == LANES
    H, W = KV_HEADS, Q_PER_KV * Q_TILE

    def win(j):
        return lambda i: (jnp.maximum(i - (WIN_CHUNKS - 1) + j, 0), 0)

    def win3(j):
        return lambda i: (jnp.maximum(i - (WIN_CHUNKS - 1) + j, 0), 0, 0)

    in_specs = [
        pl.BlockSpec((1, H, Q_PER_KV * HEAD_DIM, Q_TILE), lambda i: (i, 0, 0, 0)),
        pl.BlockSpec(kc.shape, lambda i: (0, 0, 0)),
        pl.BlockSpec(vcT.shape, lambda i: (0, 0, 0)),
        pl.BlockSpec(ks.shape, lambda i: (0, 0)),
        pl.BlockSpec(vsT.shape, lambda i: (0, 0, 0)),
    ]
    in_specs += [pl.BlockSpec((Q_TILE, KV_WIDTH), win(j)) for j in range(WIN_CHUNKS)]
    in_specs += [pl.BlockSpec((1, KV_WIDTH, Q_TILE), win3(j)) for j in range(WIN_CHUNKS)]
    in_specs += [pl.BlockSpec((1, H, 16, Q_TILE), lambda i: (i, 0, 0, 0))]
    return pl.pallas_call(
        _attn_kernel,
        grid=(n_q,),
        in_specs=in_specs,
        out_specs=pl.BlockSpec((Q_TILE, ATTN_WIDTH), lambda i: (i, 0)),
        out_shape=jax.ShapeDtypeStruct((l, ATTN_WIDTH), F32),
        scratch_shapes=[pltpu.VMEM((H, P_PAD + n_cmp, Q_TILE), F32),
                        pltpu.VMEM((n_sel, H * Q_TILE), F32),
                        pltpu.VMEM((8, H * Q_TILE), F32),
                        pltpu.VMEM((H, HEAD_DIM, W), F32)]
        + [pltpu.VMEM((KEY_CHUNK, W), F32), pltpu.VMEM((KEY_CHUNK, W), F32),
           pltpu.VMEM((KEY_CHUNK, W), BF16), pltpu.VMEM((KEY_CHUNK, W), BF16)] * H,
        compiler_params=_params(1),
        name="attn",
    )(qT, kc, vcT, ks, vsT, *([kw] * WIN_CHUNKS), *([vwT] * WIN_CHUNKS), gT)


SSM_TILE = 256
SSM_PACK = 16


def _ssm_kernel(u_ref, a_ref, wb_ref, wc_ref, msk_ref, d_ref, wg_ref, bg_ref, y_ref,
                urow_scr, x_scr, yall_scr, y_scr, st_scr):
    step = pl.program_id(0)
    tc, width = u_ref.shape
    P = SSM_PACK

    @pl.when(step == 0)
    def _():
        st_scr[...] = jnp.zeros_like(st_scr)

    PH = P // 2
    hw = width // 2
    msk = msk_ref[...]

    def spread(ti, _):
        rows = pl.ds(pl.multiple_of(ti * PH, PH), PH)
        urow = u_ref[pl.ds(ti, 1), :]
        for hf in range(2):
            urow_scr[hf, rows, :] = jnp.broadcast_to(urow[:, hf * hw:(hf + 1) * hw], (PH, hw)) * msk
        return 0

    jax.lax.fori_loop(0, tc, spread, 0, unroll=8)
    for hf in range(2):
        x_scr[hf] = _dot(urow_scr[hf].astype(BF16), wb_ref[hf * hw:(hf + 1) * hw, :])

    a_re = [a_ref[0, hf * PH:(hf + 1) * PH, :] for hf in range(2)]
    a_im = [a_ref[1, hf * PH:(hf + 1) * PH, :] for hf in range(2)]

    def one(ti, carry):
        rows = pl.ds(pl.multiple_of(ti * PH, PH), PH)
        out = []
        for hf in range(2):
            x_re, x_im = carry[hf]
            n_re = a_re[hf] * x_re - a_im[hf] * x_im + x_scr[hf, rows, 0:LANES]
            n_im = a_re[hf] * x_im + a_im[hf] * x_re + x_scr[hf, rows, LANES:2 * LANES]
            x_scr[hf, rows, 0:LANES] = n_re
            x_scr[hf, rows, LANES:2 * LANES] = n_im
            out.append((n_re, n_im))
        return tuple(out)

    init = tuple((st_scr[0, hf * PH:(hf + 1) * PH, :], st_scr[1, hf * PH:(hf + 1) * PH, :]) for hf in range(2))
    last = jax.lax.fori_loop(0, tc, one, init, unroll=8)
    for hf in range(2):
        st_scr[0, hf * PH:(hf + 1) * PH, :] = last[hf][0]
        st_scr[1, hf * PH:(hf + 1) * PH, :] = last[hf][1]
        yall_scr[hf] = _dot(x_scr[hf].astype(BF16), wc_ref[:, hf * hw:(hf + 1) * hw])

    def gather(ti, _):
        rows = pl.ds(pl.multiple_of(ti * PH, PH), PH)
        for hf in range(2):
            y_scr[pl.ds(ti, 1), hf * hw:(hf + 1) * hw] = jnp.sum(yall_scr[hf, rows, :] * msk, axis=0, keepdims=True)
        return 0

    jax.lax.fori_loop(0, tc, gather, 0, unroll=8)
    u = u_ref[...]
    y = _gelu_tanh(y_scr[...] + d_ref[...] * u)
    y_ref[...] = y * _sigmoid(_dot(y.astype(BF16), wg_ref[...]) + bg_ref[...])


def _s5(zs, lam_re, lam_im, log_step, b_re, b_im, c_re, c_im, d_skip, w_glu, b_glu):
    l, width = zs.shape
    G, Pn, C = b_re.shape
    assert Pn == SSM_STATE and C == SSM_GROUP and G * C == width and G == 2 * SSM_PACK
    step = jnp.exp(log_step)[:, None]
    mag = jnp.exp(lam_re * step)
    a_re = mag * jnp.cos(lam_im * step)
    a_im = mag * jnp.sin(lam_im * step)
    den = lam_re * lam_re + lam_im * lam_im
    n_re = a_re - 1.0
    f_re = (n_re * lam_re + a_im * lam_im) / den
    f_im = (a_im * lam_re - n_re * lam_im) / den
    bb_re = f_re[..., None] * b_re - f_im[..., None] * b_im
    bb_im = f_re[..., None] * b_im + f_im[..., None] * b_re
    a_pack = jnp.stack([a_re.reshape(SSM_PACK, 2 * Pn), a_im.reshape(SSM_PACK, 2 * Pn)])

    eye2 = jnp.eye(2, dtype=F32)

    def b_stack(bb):
        t = bb.reshape(SSM_PACK, 2, Pn, C)
        return jnp.einsum('jgpc,gh->jhcgp', t, eye2).reshape(width, 2 * Pn)

    w_b = jnp.concatenate([b_stack(bb_re), b_stack(bb_im)], axis=1).astype(BF16)

    def c_stack(cc):
        t = cc.reshape(SSM_PACK, 2, C, Pn)
        return jnp.einsum('jgcp,gh->gpjhc', t, eye2).reshape(2 * Pn, width)

    w_c = jnp.concatenate([c_stack(c_re), -c_stack(c_im)], axis=0).astype(BF16)
    half_rows, half_w = SSM_PACK // 2, width // 2
    lane_row = jnp.arange(half_w) // (2 * C)
    msk = (lane_row[None, :] == jnp.arange(half_rows)[:, None]).astype(F32)

    tc = SSM_TILE
    const2 = lambda i: (0, 0)
    return pl.pallas_call(
        _ssm_kernel,
        grid=(l // tc,),
        in_specs=[pl.BlockSpec((tc, width), lambda i: (i, 0)),
                  pl.BlockSpec(a_pack.shape, lambda i: (0, 0, 0)),
                  pl.BlockSpec(w_b.shape, const2),
                  pl.BlockSpec(w_c.shape, const2),
                  pl.BlockSpec(msk.shape, const2),
                  pl.BlockSpec((1, width), const2),
                  pl.BlockSpec((width, width), const2),
                  pl.BlockSpec((1, width), const2)],
        out_specs=pl.BlockSpec((tc, width), lambda i: (i, 0)),
        out_shape=jax.ShapeDtypeStruct((l, width), F32),
        scratch_shapes=[pltpu.VMEM((2, tc * half_rows, half_w), F32),
                        pltpu.VMEM((2, tc * half_rows, 2 * LANES), F32),
                        pltpu.VMEM((2, tc * half_rows, half_w), F32),
                        pltpu.VMEM((tc, width), F32),
                        pltpu.VMEM((2, SSM_PACK, LANES), F32)],
        compiler_params=_params(1),
        name="ssm",
    )(zs, a_pack, w_b, w_c, msk, d_skip.reshape(1, width), w_glu.astype(BF16), b_glu.reshape(1, width))


def _outproj_kernel(x_ref, oa_ref, ys_ref, ga_ref, gs_ref, wa_ref, ws_ref, gt_ref, o_ref):
    a = (_rms_scale(oa_ref[...]) * ga_ref[...]).astype(BF16)
    s = (_rms_scale(ys_ref[...]) * gs_ref[...]).astype(BF16)
    y = _dot(a, wa_ref[...]) + _dot(s, ws_ref[...])
    o_ref[...] = x_ref[...] + gt_ref[...] * y


def _out_projection(x2, o_attn, y_ssm, attn_g, ssm_g, w_o, gt):
    l, d = x2.shape
    tm = ROW_TILE
    aw, sw = o_attn.shape[1], y_ssm.shape[1]
    row = lambda i: (i, 0)
    const = lambda i: (0, 0)
    return pl.pallas_call(
        _outproj_kernel,
        grid=(l // tm,),
        in_specs=[pl.BlockSpec((tm, d), row), pl.BlockSpec((tm, aw), row), pl.BlockSpec((tm, sw), row),
                  pl.BlockSpec((1, aw), const), pl.BlockSpec((1, sw), const),
                  pl.BlockSpec((aw, d), const), pl.BlockSpec((sw, d), const), pl.BlockSpec((1, d), const)],
        out_specs=pl.BlockSpec((tm, d), row),
        out_shape=jax.ShapeDtypeStruct((l, d), F32),
        compiler_params=_params(1),
        name="outproj",
    )(x2, o_attn, y_ssm, attn_g.reshape(1, aw), ssm_g.reshape(1, sw),
      w_o[:aw].astype(BF16), w_o[aw:].astype(BF16), gt)


FFN_TILE = 256
FFN_COLS = 256
HALO = 8


def _ffn_kernel(x_ref, g_ref, sc_ref, sh_ref, wup_ref, cw_ref, cb_ref, wdn_ref, gt_ref, gf_ref,
                o_ref, ua_scr, uv_scr):
    step = pl.program_id(0)
    tm = x_ref.shape[0]
    n_col = wdn_ref.shape[0]

    @pl.when(step == 0)
    def _():
        ua_scr[...] = jnp.zeros_like(ua_scr)
        uv_scr[...] = jnp.zeros_like(uv_scr)

    x = x_ref[...]
    hb = (_rms_scale(x) * g_ref[...] * (1.0 + sc_ref[...]) + sh_ref[...]).astype(BF16)

    def conv(scr, j, u, col):
        scr[j, 0:HALO, :] = scr[j, tm:tm + HALO, :]
        scr[j, HALO:tm + HALO, :] = u
        w = cw_ref[col]
        out = cb_ref[col]
        for k in range(CONV_WIDTH):
            off = HALO - (CONV_WIDTH - 1) + k
            out = out + w[k:k + 1, :] * scr[j, off:off + tm, :]
        return out

    def col_step(j, acc):
        a = conv(ua_scr, j, _dot(hb, wup_ref[j]), j)
        v = conv(uv_scr, j, _dot(hb, wup_ref[n_col + j]), n_col + j)
        act = (a * _sigmoid(a) * v).astype(BF16)
        return acc + _dot(act, wdn_ref[j])

    y = jax.lax.fori_loop(0, n_col, col_step, jnp.zeros(x.shape, F32), unroll=True)
    x2 = x + gt_ref[...] * y
    o_ref[...] = _rms_scale(x2) * gf_ref[...]


def _conv_ffn(x1, norm_g, sc, sh, w_up, conv_w, conv_b, w_down, gt, normf_g):
    l, d = x1.shape
    f = w_down.shape[0]
    tm, cw = FFN_TILE, FFN_COLS
    n_col = f // cw
    wup = w_up.astype(BF16).reshape(d, 2 * n_col, cw).transpose(1, 0, 2)
    cwt = conv_w.reshape(CONV_WIDTH, 2 * n_col, cw).transpose(1, 0, 2)
    cbt = conv_b.reshape(2 * n_col, 1, cw)
    wdn = w_down.astype(BF16).reshape(n_col, cw, d)
    row = lambda i: (i, 0)
    const = lambda i: (0, 0)
    const3 = lambda i: (0, 0, 0)
    return pl.pallas_call(
        _ffn_kernel,
        grid=(l // tm,),
        in_specs=[pl.BlockSpec((tm, d), row),
                  pl.BlockSpec((1, d), const), pl.BlockSpec((1, d), const), pl.BlockSpec((1, d), const),
                  pl.BlockSpec(wup.shape, const3), pl.BlockSpec(cwt.shape, const3),
                  pl.BlockSpec(cbt.shape, const3), pl.BlockSpec(wdn.shape, const3),
                  pl.BlockSpec((1, d), const), pl.BlockSpec((1, d), const)],
        out_specs=pl.BlockSpec((tm, d), row),
        out_shape=jax.ShapeDtypeStruct((l, d), F32),
        scratch_shapes=[pltpu.VMEM((n_col, tm + HALO, cw), F32),
                        pltpu.VMEM((n_col, tm + HALO, cw), F32)],
        compiler_params=_params(1),
        name="ffn",
    )(x1, norm_g.reshape(1, d), sc, sh, wup, cwt, cbt, wdn, gt, normf_g.reshape(1, d))


def kernel(x, c, norm1_g, norm2_g, normf_g, w_ada, b_ada, w_in, pe_k, pe_v, w_ck1, w_ck2, w_cv1, w_cv2,
           lam_re, lam_im, log_step, b_re, b_im, c_re, c_im, d_skip, w_glu, b_glu, attn_norm_g,
           ssm_norm_g, w_o, w_up, conv_w, conv_b, w_down):
    b, l, d = x.shape
    depth = w_ada.shape[0]
    assert b == 1 and depth == 1
    assert l % KEY_CHUNK == 0 and l % ROW_TILE == 0 and l >= WINDOW
    x2 = x.reshape(l, d)
    mod = _modulation(c, w_ada[0], b_ada[0])
    sh1, sc1, gt1, sh2, sc2, gt2 = [mod[:, k * d:(k + 1) * d] for k in range(6)]

    qT, k_c, k_s, k_w, v_c, v_sT, v_wT, gT, zs = _in_projection(x2, norm1_g[0], sc1, sh1, w_in[0])
    kc = _compress(k_c, pe_k[0], w_ck1[0], w_ck2[0], transpose_out=False)
    vcT = _compress(v_c, pe_v[0], w_cv1[0], w_cv2[0], transpose_out=True)
    o_attn = _attention(qT, kc, vcT, k_s, v_sT, k_w, v_wT, gT, l)
    y_ssm = _s5(zs, lam_re[0], lam_im[0], log_step[0], b_re[0], b_im[0], c_re[0], c_im[0],
                d_skip[0], w_glu[0], b_glu[0])
    x1 = _out_projection(x2, o_attn, y_ssm, attn_norm_g[0], ssm_norm_g[0], w_o[0], gt1)
    out = _conv_ffn(x1, norm2_g[0], sc2, sh2, w_up[0], conv_w[0], conv_b[0], w_down[0], gt2, normf_g)
    return out.reshape(b, l, d)
```

```python
import functools
import math

import jax
import jax.numpy as jnp
import numpy as np
from jax.experimental import pallas as pl
from jax.experimental.pallas import tpu as pltpu

ATTN_HEADS = 8
KV_HEADS = 2
HEAD_DIM = 64
Q_PER_KV = ATTN_HEADS // KV_HEADS
ATTN_WIDTH = ATTN_HEADS * HEAD_DIM
KV_WIDTH = KV_HEADS * HEAD_DIM
SSM_GROUP = 16
SSM_STATE = 64
CMP_BLOCK = 32
CMP_STRIDE = 16
SEL_BLOCK = 64
SEL_TOPN = 16
WINDOW = 512
FORCE_SCORE = 1.0e4
ROPE_THETA = 10000.0
CONV_WIDTH = 3
NORM_EPS = 1e-6

LANES = 128
Q_TILE = 128
KEY_CHUNK = 512
ROW_TILE = 512
WIN_CHUNKS = WINDOW // Q_TILE + 1
VMEM_LIMIT = 56 * 1024 * 1024

F32 = jnp.float32
BF16 = jnp.bfloat16
NEG_INF = float("-inf")


def _params(n_axes, vmem=VMEM_LIMIT):
    return pltpu.CompilerParams(dimension_semantics=("arbitrary",) * n_axes,
                                vmem_limit_bytes=vmem)


def _split(a):
    hi = a.astype(BF16)
    lo = (a - hi.astype(F32)).astype(BF16)
    return hi, lo


def _dot(a, b):
    return jnp.dot(a, b, preferred_element_type=F32)


def _dot_hp(a, b):
    a_hi, a_lo = _split(a)
    b_hi, b_lo = _split(b)
    return _dot(a_hi, b_hi) + (_dot(a_hi, b_lo) + _dot(a_lo, b_hi))


def _sigmoid(x):
    return 1.0 / (1.0 + jnp.exp(-x))


def _gelu_tanh(x):
    c = math.sqrt(2.0 / math.pi)
    return 0.5 * x * (1.0 + jnp.tanh(c * (x + 0.044715 * (x * x * x))))


def _rms_scale(x):
    return x * jax.lax.rsqrt(jnp.mean(x * x, axis=-1, keepdims=True) + NORM_EPS)


def _mod_kernel(c_ref, w_ref, b_ref, o_ref):
    c = c_ref[...]
    s = c * _sigmoid(c)
    o_ref[...] = _dot_hp(s, w_ref[...]) + b_ref[...]


def _modulation(c, w_ada, b_ada):
    d, n = w_ada.shape
    bn = n // 4
    c8 = jnp.broadcast_to(c, (8, d))
    out = pl.pallas_call(
        _mod_kernel,
        grid=(n // bn,),
        in_specs=[pl.BlockSpec((8, d), lambda j: (0, 0)),
                  pl.BlockSpec((d, bn), lambda j: (0, j)),
                  pl.BlockSpec((1, bn), lambda j: (0, j))],
        out_specs=pl.BlockSpec((8, bn), lambda j: (0, j)),
        out_shape=jax.ShapeDtypeStruct((8, n), F32),
        compiler_params=_params(1),
        name="mod",
    )(c8, w_ada, b_ada.reshape(1, n))
    return out[0:1]


def _inproj_kernel(x_ref, g_ref, sc_ref, sh_ref, cos_ref, sin_ref, w_ref,
                   qT_ref, kc_ref, ks_ref, kw_ref, vc_ref, vsT_ref, vwT_ref, gT_ref, zs_ref):
    x = x_ref[...]
    h = _rms_scale(x) * g_ref[...] * (1.0 + sc_ref[...]) + sh_ref[...]
    hb = h.astype(BF16)
    cos = cos_ref[...]
    sin = sin_ref[...]
    n_sub = x.shape[0] // Q_TILE
    widths = (ATTN_WIDTH, ATTN_WIDTH, 3 * KV_WIDTH, 3 * KV_WIDTH, 3 * KV_WIDTH, LANES, w_ref.shape[1])
    starts = np.concatenate([[0], np.cumsum(widths[:-1])])
    wq_ref, wqr_ref, wk_ref, wkr_ref, wv_ref, wg_ref, ws_ref = [
        w_ref.at[:, int(a):int(min(a + n, w_ref.shape[1]))] for a, n in zip(starts, widths)]

    zq = _dot(hb, wq_ref[...])
    zqr = _dot(hb, wqr_ref[...])
    scale = HEAD_DIM ** -0.5 * math.log2(math.e)
    q_cols = []
    for j in range(ATTN_WIDTH // LANES):
        sl = slice(j * LANES, (j + 1) * LANES)
        q_cols.append((zq[:, sl] * cos + zqr[:, sl] * sin) * scale)
    q = jnp.concatenate(q_cols, axis=1)
    for s in range(n_sub):
        qs = q[s * Q_TILE:(s + 1) * Q_TILE, :].T
        qT_ref[s] = qs.reshape(KV_HEADS, Q_PER_KV * HEAD_DIM, Q_TILE)

    zk = _dot(hb, wk_ref[...])
    zkr = _dot(hb, wkr_ref[...])
    k_outs = (kc_ref, ks_ref, kw_ref)
    for j in range(3):
        sl = slice(j * LANES, (j + 1) * LANES)
        kj = zk[:, sl] * cos + zkr[:, sl] * sin
        k_outs[j][...] = kj.astype(k_outs[j].dtype)

    zv = _dot(hb, wv_ref[...])
    vc_ref[...] = zv[:, 0:LANES]
    vs = zv[:, LANES:2 * LANES]
    vw = zv[:, 2 * LANES:3 * LANES]
    for s in range(n_sub):
        vsT_ref[s] = vs[s * Q_TILE:(s + 1) * Q_TILE, :].T.astype(BF16)
        vwT_ref[s] = vw[s * Q_TILE:(s + 1) * Q_TILE, :].T.astype(BF16)

    gates = _sigmoid(_dot(hb, wg_ref[...]))
    for s in range(n_sub):
        gt = gates[s * Q_TILE:(s + 1) * Q_TILE, :].T
        gT_ref[s] = gt[0:2 * 16, :].reshape(KV_HEADS, 16, Q_TILE)

    zs_ref[...] = _dot(hb, ws_ref[...])


def _in_projection(x2, norm_g, sc, sh, w_in):
    l, d = x2.shape
    tm = ROW_TILE
    cuts = np.cumsum([0, ATTN_WIDTH] + [KV_WIDTH] * 6 + [3 * ATTN_HEADS])
    q_cols = np.arange(cuts[0], cuts[1])
    k_cols = np.concatenate([np.arange(cuts[j], cuts[j + 1]) for j in (1, 3, 5)])
    v_cols = np.concatenate([np.arange(cuts[j], cuts[j + 1]) for j in (2, 4, 6)])
    s_cols = np.arange(cuts[8], w_in.shape[1])
    ssm_w = len(s_cols)

    def rot(cols):
        c = cols.reshape(-1, 2, HEAD_DIM // 2)
        sign = np.concatenate([-np.ones_like(c[:, :1]), np.ones_like(c[:, :1])], axis=1)
        return c[:, ::-1].reshape(-1), sign.reshape(-1)

    g_cols = np.zeros((LANES,), np.int64)
    g_sign = np.zeros((LANES,), np.int64)
    per_kv = Q_PER_KV * 3
    for hh in range(KV_HEADS):
        g_cols[hh * 16:hh * 16 + per_kv] = cuts[7] + hh * per_kv + np.arange(per_kv)
        g_sign[hh * 16:hh * 16 + per_kv] = 1
    qr, qr_sign = rot(q_cols)
    kr, kr_sign = rot(k_cols)
    cols = np.concatenate([q_cols, qr, k_cols, kr, v_cols, g_cols, s_cols])
    sign = np.concatenate([np.ones_like(q_cols), qr_sign, np.ones_like(k_cols), kr_sign,
                           np.ones_like(v_cols), g_sign, np.ones_like(s_cols)])
    w_all = (jnp.take(w_in, jnp.asarray(cols, jnp.int32), axis=1)
             * jnp.asarray(sign, F32)[None, :]).astype(BF16)
    weights = [w_all]

    half = HEAD_DIM // 2
    inv = jnp.power(ROPE_THETA, -jnp.arange(half, dtype=F32) * 2.0 / HEAD_DIM)
    ang = jnp.arange(l, dtype=F32)[:, None] * inv[None, :]
    cos = jnp.tile(jnp.cos(ang), (1, LANES // half))
    sin = jnp.tile(jnp.sin(ang), (1, LANES // half))

    n_q = l // Q_TILE
    sub = tm // Q_TILE
    row = lambda i: (i, 0)
    const = lambda i: (0, 0)
    out_shape = [
        jax.ShapeDtypeStruct((n_q, KV_HEADS, Q_PER_KV * HEAD_DIM, Q_TILE), F32),
        jax.ShapeDtypeStruct((l, KV_WIDTH), F32),
        jax.ShapeDtypeStruct((l, KV_WIDTH), BF16),
        jax.ShapeDtypeStruct((l, KV_WIDTH), BF16),
        jax.ShapeDtypeStruct((l, KV_WIDTH), F32),
        jax.ShapeDtypeStruct((n_q, KV_WIDTH, Q_TILE), BF16),
        jax.ShapeDtypeStruct((n_q, KV_WIDTH, Q_TILE), BF16),
        jax.ShapeDtypeStruct((n_q, KV_HEADS, 16, Q_TILE), F32),
        jax.ShapeDtypeStruct((l, ssm_w), F32),
    ]
    out_specs = [
        pl.BlockSpec((sub, KV_HEADS, Q_PER_KV * HEAD_DIM, Q_TILE), lambda i: (i, 0, 0, 0)),
        pl.BlockSpec((tm, KV_WIDTH), row),
        pl.BlockSpec((tm, KV_WIDTH), row),
        pl.BlockSpec((tm, KV_WIDTH), row),
        pl.BlockSpec((tm, KV_WIDTH), row),
        pl.BlockSpec((sub, KV_WIDTH, Q_TILE), lambda i: (i, 0, 0)),
        pl.BlockSpec((sub, KV_WIDTH, Q_TILE), lambda i: (i, 0, 0)),
        pl.BlockSpec((sub, KV_HEADS, 16, Q_TILE), lambda i: (i, 0, 0, 0)),
        pl.BlockSpec((tm, ssm_w), row),
    ]
    in_specs = [pl.BlockSpec((tm, d), row),
                pl.BlockSpec((1, d), const), pl.BlockSpec((1, d), const), pl.BlockSpec((1, d), const),
                pl.BlockSpec((tm, LANES), row), pl.BlockSpec((tm, LANES), row)]
    in_specs += [pl.BlockSpec(w.shape, const) for w in weights]
    return pl.pallas_call(
        _inproj_kernel,
        grid=(l // tm,),
        in_specs=in_specs,
        out_specs=out_specs,
        out_shape=out_shape,
        compiler_params=_params(1),
        name="inproj",
    )(x2, norm_g.reshape(1, d), sc, sh, cos, sin, *weights)


def _compress_kernel(z_ref, pe_ref, w1_ref, w1p_ref, w2_ref, o_ref, *, transpose_out):
    n = z_ref.shape[0] // CMP_STRIDE
    first = jnp.zeros((n, w1_ref.shape[1]), F32)
    second = jnp.zeros((n, w1_ref.shape[1]), F32)
    for j in range(CMP_STRIDE):
        x_j = z_ref[pl.ds(j, n, stride=CMP_STRIDE), :].astype(BF16)
        first = first + _dot(x_j, w1p_ref[0, j])
        second = second + _dot(x_j, w1p_ref[0, CMP_STRIDE + j])
    bias = _dot(pe_ref[...].astype(BF16), w1_ref[...])[0:1]
    h1 = first + pltpu.roll(second, n - 1, 0) + bias
    out = _dot_hp(_gelu_tanh(h1), w2_ref[...])
    if transpose_out:
        o_ref[0] = out.T[0:HEAD_DIM, :]
    else:
        o_ref[0] = out[:, 0:HEAD_DIM]


def _compress(z, pe, w1, w2, transpose_out):
    l = z.shape[0]
    n = l // CMP_STRIDE
    hidden = w1.shape[1]
    assert CMP_BLOCK == 2 * CMP_STRIDE
    pe8 = jnp.broadcast_to(pe.reshape(1, CMP_BLOCK * HEAD_DIM), (8, CMP_BLOCK * HEAD_DIM))
    w1b = w1.astype(BF16)
    w1r = w1b.reshape(CMP_BLOCK, 1, HEAD_DIM, hidden)
    head_sel = jnp.eye(KV_HEADS, dtype=BF16)[:, None, :, None, None]
    w1p = (head_sel * w1r[None]).reshape(KV_HEADS, CMP_BLOCK, KV_WIDTH, hidden)
    w2p = jnp.pad(w2, ((0, 0), (0, LANES - HEAD_DIM)))
    oshape = (KV_HEADS, HEAD_DIM, n) if transpose_out else (KV_HEADS, n, HEAD_DIM)
    return pl.pallas_call(
        functools.partial(_compress_kernel, transpose_out=transpose_out),
        grid=(KV_HEADS,),
        in_specs=[pl.BlockSpec(z.shape, lambda h: (0, 0)),
                  pl.BlockSpec(pe8.shape, lambda h: (0, 0)),
                  pl.BlockSpec(w1b.shape, lambda h: (0, 0)),
                  pl.BlockSpec((1,) + w1p.shape[1:], lambda h: (h, 0, 0, 0)),
                  pl.BlockSpec(w2p.shape, lambda h: (0, 0))],
        out_specs=pl.BlockSpec((1,) + oshape[1:], lambda h: (h, 0, 0)),
        out_shape=jax.ShapeDtypeStruct(oshape, F32),
        compiler_params=_params(1),
        name="compress_v" if transpose_out else "compress_k",
    )(z, pe8, w1b, w1p, w2p)


def _tile4(a):
    return jnp.concatenate([a] * Q_PER_KV, axis=1)


MASKED = -1e30
P_PAD = 8


SEL_STEP = 64
N_FORCED = 3


def _attn_kernel(qT_ref, kc_ref, vcT_ref, ks_ref, vsT_ref, *rest):
    kw_refs = rest[0:WIN_CHUNKS]
    vw_refs = rest[WIN_CHUNKS:2 * WIN_CHUNKS]
    gT_ref, o_ref, p_scr, bias_scr, own_scr, oc_scr = rest[2 * WIN_CHUNKS:2 * WIN_CHUNKS + 6]
    per_head = rest[2 * WIN_CHUNKS + 6:]
    s_even, s_odd, p_even, p_odd = per_head[0::4], per_head[1::4], per_head[2::4], per_head[3::4]
    i = pl.program_id(0)
    n_sel = bias_scr.shape[0]
    H, G, DH, TQ = KV_HEADS, Q_PER_KV, HEAD_DIM, Q_TILE
    W = G * TQ
    heads = range(H)

    q_all, q_pad = [], []
    for h in heads:
        qT = qT_ref[0, h]
        qa = jnp.concatenate([qT[g * DH:(g + 1) * DH, :] for g in range(G)], axis=1)
        zeros = jnp.zeros_like(qa)
        q_all.append(qa)
        q_pad.append(jnp.concatenate([qa if hh == h else zeros for hh in heads], axis=0).astype(BF16))
    v_rows = [slice(h * DH, (h + 1) * DH) for h in heads]

    t = i * TQ + jax.lax.broadcasted_iota(jnp.int32, (1, TQ), 1)
    cur = jax.lax.shift_right_logical(t, int(math.log2(SEL_BLOCK)))

    ratio = SEL_BLOCK // CMP_STRIDE

    def select(ns):
        nc = ns * ratio
        c_end = jax.lax.broadcasted_iota(jnp.int32, (nc, 1), 0) * CMP_STRIDE + (CMP_BLOCK - 1)
        c_bias = _tile4(jnp.where(c_end <= t, 0.0, MASKED))
        blk = jax.lax.broadcasted_iota(jnp.int32, (ns, 1), 0)
        blk_f = blk.astype(F32)
        forced = (blk == 0) | (blk == cur) | (blk == cur - 1)
        scores = []
        for h in heads:
            s = _dot(kc_ref[h, 0:nc, :].astype(BF16), q_all[h].astype(BF16)) + c_bias
            m = jnp.max(s, axis=0, keepdims=True)
            e = jnp.exp2(s - m)
            any_valid = jnp.where(m > 0.5 * MASKED, 1.0, 0.0)
            p = e * (any_valid / jnp.sum(e, axis=0, keepdims=True))
            oc_scr[h] = _dot(vcT_ref[h, :, 0:nc].astype(BF16), p.astype(BF16))
            p_sum = p[:, 0:TQ]
            for g in range(1, G):
                p_sum = p_sum + p[:, g * TQ:(g + 1) * TQ]
            p_scr[h, 0:P_PAD, :] = jnp.zeros((P_PAD, TQ), F32)
            p_scr[h, P_PAD:P_PAD + nc, :] = p_sum

            def every(off):
                return p_scr[h, pl.ds(P_PAD + off, ns, stride=ratio), :]

            imp = 0.5 * (every(-1) + every(3)) + (every(0) + every(1) + every(2))
            scores.append(jnp.where((blk > cur) | forced, NEG_INF, imp))
        score = jnp.concatenate(scores, axis=1)

        def pick_one(_, score):
            m = jnp.max(score, axis=0, keepdims=True)
            first = jnp.min(jnp.where(score == m, blk_f, float(ns)), axis=0, keepdims=True)
            return jnp.where(blk_f == first, NEG_INF, score)

        left = jax.lax.fori_loop(0, SEL_TOPN - N_FORCED, pick_one, score)
        cur_w = jnp.concatenate([cur] * H, axis=1)
        bias = jnp.where((left == NEG_INF) & (blk <= cur_w), 0.0, MASKED)
        own_scr[0:1, :] = jnp.max(jnp.where(blk == 2 * i, bias, MASKED), axis=0, keepdims=True)
        own_scr[1:2, :] = jnp.max(jnp.where(blk == 2 * i + 1, bias, MASKED), axis=0, keepdims=True)
        bias_scr[0:ns, :] = jnp.where((blk == 2 * i) | (blk == 2 * i + 1), MASKED, bias)
        if ns < n_sel:
            bias_scr[ns:n_sel, :] = jnp.full((n_sel - ns, H * TQ), MASKED, F32)

    sizes = [min(n_sel, SEL_STEP * (k + 1)) for k in range(pl.cdiv(n_sel, SEL_STEP))]
    variant = jnp.minimum((2 * i + 1) // SEL_STEP, len(sizes) - 1)
    for k, ns in enumerate(sizes):
        pl.when(variant == k)(functools.partial(select, ns))

    KC = KEY_CHUNK
    BPC = KC // SEL_BLOCK
    VROWS = DH + 16
    onehot = (jax.lax.broadcasted_iota(jnp.int32, (KC, LANES), 1)
              == jax.lax.shift_right_logical(jax.lax.broadcasted_iota(jnp.int32, (KC, LANES), 0),
                                             int(math.log2(SEL_BLOCK)))).astype(BF16)
    ones_rows = jnp.ones((VROWS - DH, KC), BF16)
    pad_rows = jnp.zeros((LANES - BPC, W), F32)
    last_chunk = ks_ref.shape[0] // KC - 1

    def augmented_q(c):
        rows = bias_scr[pl.ds(pl.multiple_of(c * BPC, BPC), BPC), :]
        out = []
        for h in heads:
            extra = jnp.concatenate([_tile4(rows[:, h * TQ:(h + 1) * TQ]), pad_rows], axis=0)
            out.append(jnp.concatenate([q_pad[h], extra.astype(BF16)], axis=0))
        return out

    def scores(c):
        k = ks_ref[pl.ds(pl.multiple_of(c * KC, KC), KC), :]
        k_aug = jnp.concatenate([k, onehot], axis=1)
        q_aug = augmented_q(c)
        out = []
        for h in heads:
            s = _dot(k_aug, q_aug[h])
            out.append((s, jnp.max(s, axis=0, keepdims=True)))
        return out

    def values(c):
        v = jnp.concatenate([vsT_ref[(KC // TQ) * c + j] for j in range(KC // TQ)], axis=1)
        return [jnp.concatenate([v[v_rows[h], :], ones_rows], axis=0) for h in heads]

    def chunk_step(c, m_run, acc, mx_cur, s_cur, s_next, p_prev, p_cur):
        v_prev = values(jnp.maximum(c - 1, 0))
        nxt = scores(jnp.minimum(c + 1, last_chunk))
        m_new, mx_next = [], []
        for h in heads:
            acc[h] = acc[h] + _dot(v_prev[h], p_prev[h][...])
            m_new.append(jnp.maximum(m_run[h], mx_cur[h]))
            acc[h] = jnp.exp2(m_run[h] - m_new[h]) * acc[h]
            p_cur[h][...] = jnp.exp2(s_cur[h][...] - m_new[h]).astype(BF16)
            s_next[h][...] = nxt[h][0]
            mx_next.append(nxt[h][1])
        return m_new, acc, mx_next

    def pair(j, carry):
        m_run, acc, mx = (list(c) for c in carry)
        m_run, acc, mx = chunk_step(2 * j, m_run, acc, mx, s_even, s_odd, p_odd, p_even)
        m_run, acc, mx = chunk_step(2 * j + 1, m_run, acc, mx, s_odd, s_even, p_even, p_odd)
        return tuple(m_run), tuple(acc), tuple(mx)

    n_pairs = ((i + KC // TQ - 1) // (KC // TQ) + 1) // 2
    s_0 = scores(0)
    for h in heads:
        s_even[h][...] = s_0[h][0]
        p_odd[h][...] = jnp.zeros(p_odd[h].shape, BF16)
    init = (tuple(jnp.full((1, W), MASKED, F32) for _ in heads),
            tuple(jnp.zeros((VROWS, W), F32) for _ in heads),
            tuple(s_0[h][1] for h in heads))
    m_run, acc, _ = jax.lax.fori_loop(0, n_pairs, pair, init)
    v_last = values(jnp.maximum(2 * n_pairs - 1, 0))
    acc = tuple(acc[h] + _dot(v_last[h], p_odd[h][...]) for h in heads)

    row_o = jax.lax.broadcasted_iota(jnp.int32, (TQ, 1), 0)
    lane_o = jax.lax.broadcasted_iota(jnp.int32, (1, TQ), 1)
    k_own = ks_ref[pl.ds(pl.multiple_of(i * TQ, TQ), TQ), :]
    v_own = vsT_ref[i]
    k_w = jnp.concatenate([r[...] for r in kw_refs], axis=0)
    v_w = jnp.concatenate([r[0] for r in vw_refs], axis=1)
    w_pos = (i - (WIN_CHUNKS - 1)) * TQ + jax.lax.broadcasted_iota(jnp.int32, (WIN_CHUNKS * TQ, 1), 0)
    w_bias = _tile4(jnp.where((w_pos <= t) & (w_pos > t - WINDOW) & (w_pos >= 0), 0.0, MASKED))
    ones_w = jnp.ones((VROWS - DH, WIN_CHUNKS * TQ), BF16)

    head_out = []
    for h in heads:
        a = acc[h]
        own_lo = own_scr[0:1, h * TQ:(h + 1) * TQ]
        own_hi = own_scr[1:2, h * TQ:(h + 1) * TQ]
        own_bias = jnp.where(row_o <= lane_o, jnp.where(row_o < SEL_BLOCK, own_lo, own_hi), MASKED)
        s = _dot(k_own, q_pad[h]) + _tile4(own_bias)
        m_s = jnp.maximum(m_run[h], jnp.max(s, axis=0, keepdims=True))
        p = jnp.exp2(s - m_s).astype(BF16)
        a = jnp.exp2(m_run[h] - m_s) * a + _dot(jnp.concatenate([v_own[v_rows[h], :], ones_rows[:, 0:TQ]], axis=0), p)
        o_s = a[0:DH, :] / a[DH:DH + 1, :]

        s = _dot(k_w, q_pad[h]) + w_bias
        p = jnp.exp2(s - jnp.max(s, axis=0, keepdims=True)).astype(BF16)
        a_w = _dot(jnp.concatenate([v_w[v_rows[h], :], ones_w], axis=0), p)
        o_w = a_w[0:DH, :] / a_w[DH:DH + 1, :]

        o_c = oc_scr[h]
        gates = gT_ref[0, h]
        for g in range(G):
            cols = slice(g * TQ, (g + 1) * TQ)
            head_out.append(gates[3 * g:3 * g + 1, :] * o_c[:, cols]
                            + gates[3 * g + 1:3 * g + 2, :] * o_s[:, cols]
                            + gates[3 * g + 2:3 * g + 3, :] * o_w[:, cols])
    o_ref[...] = jnp.concatenate(head_out, axis=0).T


def _attention(qT, kc, vcT, ks, vsT, kw, vwT, gT, l):
    n_q = l // Q_TILE
    n_cmp = l // CMP_STRIDE
    n_sel = l // SEL_BLOCK
    assert CMP_BLOCK == 2 * CMP_STRIDE and SEL_BLOCK == 4 * CMP_STRIDE

    assert n_sel >= SEL_TOPN and KV_WIDTH == LANES
    H, W = KV_HEADS, Q_PER_KV * Q_TILE

    def win(j):
        return lambda i: (jnp.maximum(i - (WIN_CHUNKS - 1) + j, 0), 0)

    def win3(j):
        return lambda i: (jnp.maximum(i - (WIN_CHUNKS - 1) + j, 0), 0, 0)

    in_specs = [
        pl.BlockSpec((1, H, Q_PER_KV * HEAD_DIM, Q_TILE), lambda i: (i, 0, 0, 0)),
        pl.BlockSpec(kc.shape, lambda i: (0, 0, 0)),
        pl.BlockSpec(vcT.shape, lambda i: (0, 0, 0)),
        pl.BlockSpec(ks.shape, lambda i: (0, 0)),
        pl.BlockSpec(vsT.shape, lambda i: (0, 0, 0)),
    ]
    in_specs += [pl.BlockSpec((Q_TILE, KV_WIDTH), win(j)) for j in range(WIN_CHUNKS)]
    in_specs += [pl.BlockSpec((1, KV_WIDTH, Q_TILE), win3(j)) for j in range(WIN_CHUNKS)]
    in_specs += [pl.BlockSpec((1, H, 16, Q_TILE), lambda i: (i, 0, 0, 0))]
    return pl.pallas_call(
        _attn_kernel,
        grid=(n_q,),
        in_specs=in_specs,
        out_specs=pl.BlockSpec((Q_TILE, ATTN_WIDTH), lambda i: (i, 0)),
        out_shape=jax.ShapeDtypeStruct((l, ATTN_WIDTH), F32),
        scratch_shapes=[pltpu.VMEM((H, P_PAD + n_cmp, Q_TILE), F32),
                        pltpu.VMEM((n_sel, H * Q_TILE), F32),
                        pltpu.VMEM((8, H * Q_TILE), F32),
                        pltpu.VMEM((H, HEAD_DIM, W), F32)]
        + [pltpu.VMEM((KEY_CHUNK, W), F32), pltpu.VMEM((KEY_CHUNK, W), F32),
           pltpu.VMEM((KEY_CHUNK, W), BF16), pltpu.VMEM((KEY_CHUNK, W), BF16)] * H,
        compiler_params=_params(1),
        name="attn",
    )(qT, kc, vcT, ks, vsT, *([kw] * WIN_CHUNKS), *([vwT] * WIN_CHUNKS), gT)


SSM_TILE = 256
SSM_PACK = 16


def _ssm_kernel(u_ref, a_ref, wb_ref, wc_ref, msk_ref, d_ref, wg_ref, bg_ref, y_ref, x_scr, st_scr):
    step = pl.program_id(0)
    tc, width = u_ref.shape
    P = SSM_PACK

    @pl.when(step == 0)
    def _():
        st_scr[...] = jnp.zeros_like(st_scr)

    PH = P // 2
    hw = width // 2
    msk = msk_ref[...]
    u = u_ref[...]

    for hf in range(2):
        u_half = u[:, hf * hw:(hf + 1) * hw]
        wb_half = wb_ref[hf * hw:(hf + 1) * hw, :]
        for j in range(PH):
            bu = _dot((u_half * msk[j:j + 1, :]).astype(BF16), wb_half)
            x_scr[hf, 0, pl.ds(j, tc, stride=PH), :] = bu[:, 0:LANES]
            x_scr[hf, 1, pl.ds(j, tc, stride=PH), :] = bu[:, LANES:2 * LANES]

    a_re = [a_ref[0, hf * PH:(hf + 1) * PH, :] for hf in range(2)]
    a_im = [a_ref[1, hf * PH:(hf + 1) * PH, :] for hf in range(2)]

    def one(ti, carry):
        rows = pl.ds(pl.multiple_of(ti * PH, PH), PH)
        out = []
        for hf in range(2):
            x_re, x_im = carry[hf]
            n_re = a_re[hf] * x_re - a_im[hf] * x_im + x_scr[hf, 0, rows, :]
            n_im = a_re[hf] * x_im + a_im[hf] * x_re + x_scr[hf, 1, rows, :]
            x_scr[hf, 0, rows, :] = n_re
            x_scr[hf, 1, rows, :] = n_im
            out.append((n_re, n_im))
        return tuple(out)

    init = tuple((st_scr[0, hf * PH:(hf + 1) * PH, :], st_scr[1, hf * PH:(hf + 1) * PH, :]) for hf in range(2))
    last = jax.lax.fori_loop(0, tc, one, init, unroll=8)
    for hf in range(2):
        st_scr[0, hf * PH:(hf + 1) * PH, :] = last[hf][0]
        st_scr[1, hf * PH:(hf + 1) * PH, :] = last[hf][1]

    halves = []
    for hf in range(2):
        wc_half = wc_ref[:, hf * hw:(hf + 1) * hw]
        y_half = jnp.zeros((tc, hw), F32)
        for j in range(PH):
            x_j = jnp.concatenate([x_scr[hf, 0, pl.ds(j, tc, stride=PH), :],
                                   x_scr[hf, 1, pl.ds(j, tc, stride=PH), :]], axis=1).astype(BF16)
            y_half = y_half + msk[j:j + 1, :] * _dot(x_j, wc_half)
        halves.append(y_half)
    y = _gelu_tanh(jnp.concatenate(halves, axis=1) + d_ref[...] * u)
    y_ref[...] = y * _sigmoid(_dot(y.astype(BF16), wg_ref[...]) + bg_ref[...])


def _s5(zs, lam_re, lam_im, log_step, b_re, b_im, c_re, c_im, d_skip, w_glu, b_glu):
    l, width = zs.shape
    G, Pn, C = b_re.shape
    assert Pn == SSM_STATE and C == SSM_GROUP and G * C == width and G == 2 * SSM_PACK
    step = jnp.exp(log_step)[:, None]
    mag = jnp.exp(lam_re * step)
    a_re = mag * jnp.cos(lam_im * step)
    a_im = mag * jnp.sin(lam_im * step)
    den = lam_re * lam_re + lam_im * lam_im
    n_re = a_re - 1.0
    f_re = (n_re * lam_re + a_im * lam_im) / den
    f_im = (a_im * lam_re - n_re * lam_im) / den
    bb_re = f_re[..., None] * b_re - f_im[..., None] * b_im
    bb_im = f_re[..., None] * b_im + f_im[..., None] * b_re
    a_pack = jnp.stack([a_re.reshape(SSM_PACK, 2 * Pn), a_im.reshape(SSM_PACK, 2 * Pn)])

    eye2 = jnp.eye(2, dtype=F32)

    def b_stack(bb):
        t = bb.reshape(SSM_PACK, 2, Pn, C)
        return jnp.einsum('jgpc,gh->jhcgp', t, eye2).reshape(width, 2 * Pn)

    w_b = jnp.concatenate([b_stack(bb_re), b_stack(bb_im)], axis=1).astype(BF16)

    def c_stack(cc):
        t = cc.reshape(SSM_PACK, 2, C, Pn)
        return jnp.einsum('jgcp,gh->gpjhc', t, eye2).reshape(2 * Pn, width)

    w_c = jnp.concatenate([c_stack(c_re), -c_stack(c_im)], axis=0).astype(BF16)
    half_rows, half_w = SSM_PACK // 2, width // 2
    lane_row = jnp.arange(half_w) // (2 * C)
    msk = (lane_row[None, :] == jnp.arange(half_rows)[:, None]).astype(F32)

    tc = SSM_TILE
    const2 = lambda i: (0, 0)
    return pl.pallas_call(
        _ssm_kernel,
        grid=(l // tc,),
        in_specs=[pl.BlockSpec((tc, width), lambda i: (i, 0)),
                  pl.BlockSpec(a_pack.shape, lambda i: (0, 0, 0)),
                  pl.BlockSpec(w_b.shape, const2),
                  pl.BlockSpec(w_c.shape, const2),
                  pl.BlockSpec(msk.shape, const2),
                  pl.BlockSpec((1, width), const2),
                  pl.BlockSpec((width, width), const2),
                  pl.BlockSpec((1, width), const2)],
        out_specs=pl.BlockSpec((tc, width), lambda i: (i, 0)),
        out_shape=jax.ShapeDtypeStruct((l, width), F32),
        scratch_shapes=[pltpu.VMEM((2, 2, tc * half_rows, LANES), F32),
                        pltpu.VMEM((2, SSM_PACK, LANES), F32)],
        compiler_params=_params(1),
        name="ssm",
    )(zs, a_pack, w_b, w_c, msk, d_skip.reshape(1, width), w_glu.astype(BF16), b_glu.reshape(1, width))


def _outproj_kernel(x_ref, oa_ref, ys_ref, ga_ref, gs_ref, wa_ref, ws_ref, gt_ref, o_ref):
    a = (_rms_scale(oa_ref[...]) * ga_ref[...]).astype(BF16)
    s = (_rms_scale(ys_ref[...]) * gs_ref[...]).astype(BF16)
    y = _dot(a, wa_ref[...]) + _dot(s, ws_ref[...])
    o_ref[...] = x_ref[...] + gt_ref[...] * y


def _out_projection(x2, o_attn, y_ssm, attn_g, ssm_g, w_o, gt):
    l, d = x2.shape
    tm = ROW_TILE
    aw, sw = o_attn.shape[1], y_ssm.shape[1]
    row = lambda i: (i, 0)
    const = lambda i: (0, 0)
    return pl.pallas_call(
        _outproj_kernel,
        grid=(l // tm,),
        in_specs=[pl.BlockSpec((tm, d), row), pl.BlockSpec((tm, aw), row), pl.BlockSpec((tm, sw), row),
                  pl.BlockSpec((1, aw), const), pl.BlockSpec((1, sw), const),
                  pl.BlockSpec((aw, d), const), pl.BlockSpec((sw, d), const), pl.BlockSpec((1, d), const)],
        out_specs=pl.BlockSpec((tm, d), row),
        out_shape=jax.ShapeDtypeStruct((l, d), F32),
        compiler_params=_params(1),
        name="outproj",
    )(x2, o_attn, y_ssm, attn_g.reshape(1, aw), ssm_g.reshape(1, sw),
      w_o[:aw].astype(BF16), w_o[aw:].astype(BF16), gt)


FFN_TILE = 256
FFN_COLS = 256
HALO = 8


def _ffn_kernel(x_ref, g_ref, sc_ref, sh_ref, wup_ref, cw_ref, cb_ref, wdn_ref, gt_ref, gf_ref,
                o_ref, ua_scr, uv_scr):
    step = pl.program_id(0)
    tm = x_ref.shape[0]
    n_col = wdn_ref.shape[0] // FFN_COLS

    @pl.when(step == 0)
    def _():
        ua_scr[...] = jnp.zeros_like(ua_scr)
        uv_scr[...] = jnp.zeros_like(uv_scr)

    x = x_ref[...]
    hb = (_rms_scale(x) * g_ref[...] * (1.0 + sc_ref[...]) + sh_ref[...]).astype(BF16)

    cw = FFN_COLS

    def conv(scr, j, col):
        cols = slice(col * cw, (col + 1) * cw)
        scr[j, 0:HALO, :] = scr[j, tm:tm + HALO, :]
        scr[j, HALO:tm + HALO, :] = _dot(hb, wup_ref[:, cols])
        w = cw_ref[:, cols]
        out = cb_ref[:, cols]
        for k in range(CONV_WIDTH):
            off = HALO - (CONV_WIDTH - 1) + k
            out = out + w[k:k + 1, :] * scr[j, off:off + tm, :]
        return out

    y = jnp.zeros(x.shape, F32)
    for j in range(n_col):
        a = conv(ua_scr, j, j)
        v = conv(uv_scr, j, n_col + j)
        act = (a * _sigmoid(a) * v).astype(BF16)
        y = y + _dot(act, wdn_ref[j * cw:(j + 1) * cw, :])
    x2 = x + gt_ref[...] * y
    o_ref[...] = _rms_scale(x2) * gf_ref[...]


def _conv_ffn(x1, norm_g, sc, sh, w_up, conv_w, conv_b, w_down, gt, normf_g):
    l, d = x1.shape
    f = w_down.shape[0]
    tm, cw = FFN_TILE, FFN_COLS
    n_col = f // cw
    assert f % cw == 0
    wup = w_up.astype(BF16)
    cbt = conv_b.reshape(1, 2 * f)
    wdn = w_down.astype(BF16)
    row = lambda i: (i, 0)
    const = lambda i: (0, 0)
    return pl.pallas_call(
        _ffn_kernel,
        grid=(l // tm,),
        in_specs=[pl.BlockSpec((tm, d), row),
                  pl.BlockSpec((1, d), const), pl.BlockSpec((1, d), const), pl.BlockSpec((1, d), const),
                  pl.BlockSpec(wup.shape, const), pl.BlockSpec(conv_w.shape, const),
                  pl.BlockSpec(cbt.shape, const), pl.BlockSpec(wdn.shape, const),
                  pl.BlockSpec((1, d), const), pl.BlockSpec((1, d), const)],
        out_specs=pl.BlockSpec((tm, d), row),
        out_shape=jax.ShapeDtypeStruct((l, d), F32),
        scratch_shapes=[pltpu.VMEM((n_col, tm + HALO, cw), F32),
                        pltpu.VMEM((n_col, tm + HALO, cw), F32)],
        compiler_params=_params(1),
        name="ffn",
    )(x1, norm_g.reshape(1, d), sc, sh, wup, conv_w, cbt, wdn, gt, normf_g.reshape(1, d))


def kernel(x, c, norm1_g, norm2_g, normf_g, w_ada, b_ada, w_in, pe_k, pe_v, w_ck1, w_ck2, w_cv1, w_cv2,
           lam_re, lam_im, log_step, b_re, b_im, c_re, c_im, d_skip, w_glu, b_glu, attn_norm_g,
           ssm_norm_g, w_o, w_up, conv_w, conv_b, w_down):
    b, l, d = x.shape
    depth = w_ada.shape[0]
    assert b == 1 and depth == 1
    assert l % KEY_CHUNK == 0 and l % ROW_TILE == 0 and l >= WINDOW
    x2 = x.reshape(l, d)
    mod = _modulation(c, w_ada[0], b_ada[0])
    sh1, sc1, gt1, sh2, sc2, gt2 = [mod[:, k * d:(k + 1) * d] for k in range(6)]

    qT, k_c, k_s, k_w, v_c, v_sT, v_wT, gT, zs = _in_projection(x2, norm1_g[0], sc1, sh1, w_in[0])
    kc = _compress(k_c, pe_k[0], w_ck1[0], w_ck2[0], transpose_out=False)
    vcT = _compress(v_c, pe_v[0], w_cv1[0], w_cv2[0], transpose_out=True)
    o_attn = _attention(qT, kc, vcT, k_s, v_sT, k_w, v_wT, gT, l)
    y_ssm = _s5(zs, lam_re[0], lam_im[0], log_step[0], b_re[0], b_im[0], c_re[0], c_im[0],
                d_skip[0], w_glu[0], b_glu[0])
    x1 = _out_projection(x2, o_attn, y_ssm, attn_norm_g[0], ssm_norm_g[0], w_o[0], gt1)
    out = _conv_ffn(x1, norm2_g[0], sc2, sh2, w_up[0], conv_w[0], conv_b[0], w_down[0], gt2, normf_g)
    return out.reshape(b, l, d)
```

```python
import functools
import math

import jax
import jax.numpy as jnp
import numpy as np
from jax.experimental import pallas as pl
from jax.experimental.pallas import tpu as pltpu

ATTN_HEADS = 8
KV_HEADS = 2
HEAD_DIM = 64
Q_PER_KV = ATTN_HEADS // KV_HEADS
ATTN_WIDTH = ATTN_HEADS * HEAD_DIM
KV_WIDTH = KV_HEADS * HEAD_DIM
SSM_GROUP = 16
SSM_STATE = 64
CMP_BLOCK = 32
CMP_STRIDE = 16
SEL_BLOCK = 64
SEL_TOPN = 16
WINDOW = 512
FORCE_SCORE = 1.0e4
ROPE_THETA = 10000.0
CONV_WIDTH = 3
NORM_EPS = 1e-6

LANES = 128
Q_TILE = 128
KEY_CHUNK = 512
ROW_TILE = 512
WIN_CHUNKS = WINDOW // Q_TILE + 1
VMEM_LIMIT = 56 * 1024 * 1024

F32 = jnp.float32
BF16 = jnp.bfloat16
NEG_INF = float("-inf")


def _params(n_axes, vmem=VMEM_LIMIT):
    return pltpu.CompilerParams(dimension_semantics=("arbitrary",) * n_axes,
                                vmem_limit_bytes=vmem)


def _split(a):
    hi = a.astype(BF16)
    lo = (a - hi.astype(F32)).astype(BF16)
    return hi, lo


def _dot(a, b):
    return jnp.dot(a, b, preferred_element_type=F32)


def _dot_hp(a, b):
    a_hi, a_lo = _split(a)
    b_hi, b_lo = _split(b)
    return _dot(a_hi, b_hi) + (_dot(a_hi, b_lo) + _dot(a_lo, b_hi))


def _sigmoid(x):
    return 1.0 / (1.0 + jnp.exp(-x))


def _gelu_tanh(x):
    c = math.sqrt(2.0 / math.pi)
    return 0.5 * x * (1.0 + jnp.tanh(c * (x + 0.044715 * (x * x * x))))


def _rms_scale(x):
    return x * jax.lax.rsqrt(jnp.mean(x * x, axis=-1, keepdims=True) + NORM_EPS)


def _mod_kernel(c_ref, w_ref, b_ref, o_ref):
    c = c_ref[...]
    s = c * _sigmoid(c)
    o_ref[...] = _dot_hp(s, w_ref[...]) + b_ref[...]


def _modulation(c, w_ada, b_ada):
    d, n = w_ada.shape
    bn = n // 4
    c8 = jnp.broadcast_to(c, (8, d))
    out = pl.pallas_call(
        _mod_kernel,
        grid=(n // bn,),
        in_specs=[pl.BlockSpec((8, d), lambda j: (0, 0)),
                  pl.BlockSpec((d, bn), lambda j: (0, j)),
                  pl.BlockSpec((1, bn), lambda j: (0, j))],
        out_specs=pl.BlockSpec((8, bn), lambda j: (0, j)),
        out_shape=jax.ShapeDtypeStruct((8, n), F32),
        compiler_params=_params(1),
        name="mod",
    )(c8, w_ada, b_ada.reshape(1, n))
    return out[0:1]


def _inproj_kernel(x_ref, g_ref, sc_ref, sh_ref, cos_ref, sin_ref, w_ref,
                   qT_ref, kc_ref, ks_ref, kw_ref, vc_ref, vsT_ref, vwT_ref, gT_ref, zs_ref):
    x = x_ref[...]
    h = _rms_scale(x) * g_ref[...] * (1.0 + sc_ref[...]) + sh_ref[...]
    hb = h.astype(BF16)
    cos = cos_ref[...]
    sin = sin_ref[...]
    n_sub = x.shape[0] // Q_TILE
    widths = (ATTN_WIDTH, ATTN_WIDTH, 3 * KV_WIDTH, 3 * KV_WIDTH, 3 * KV_WIDTH, LANES, w_ref.shape[1])
    starts = np.concatenate([[0], np.cumsum(widths[:-1])])
    wq_ref, wqr_ref, wk_ref, wkr_ref, wv_ref, wg_ref, ws_ref = [
        w_ref.at[:, int(a):int(min(a + n, w_ref.shape[1]))] for a, n in zip(starts, widths)]

    zq = _dot(hb, wq_ref[...])
    zqr = _dot(hb, wqr_ref[...])
    scale = HEAD_DIM ** -0.5 * math.log2(math.e)
    q_cols = []
    for j in range(ATTN_WIDTH // LANES):
        sl = slice(j * LANES, (j + 1) * LANES)
        q_cols.append((zq[:, sl] * cos + zqr[:, sl] * sin) * scale)
    q = jnp.concatenate(q_cols, axis=1)
    for s in range(n_sub):
        qs = q[s * Q_TILE:(s + 1) * Q_TILE, :].T
        qT_ref[s] = qs.reshape(KV_HEADS, Q_PER_KV * HEAD_DIM, Q_TILE)

    zk = _dot(hb, wk_ref[...])
    zkr = _dot(hb, wkr_ref[...])
    k_outs = (kc_ref, ks_ref, kw_ref)
    for j in range(3):
        sl = slice(j * LANES, (j + 1) * LANES)
        kj = zk[:, sl] * cos + zkr[:, sl] * sin
        k_outs[j][...] = kj.astype(k_outs[j].dtype)

    zv = _dot(hb, wv_ref[...])
    vc_ref[...] = zv[:, 0:LANES]
    vs = zv[:, LANES:2 * LANES]
    vw = zv[:, 2 * LANES:3 * LANES]
    for s in range(n_sub):
        vsT_ref[s] = vs[s * Q_TILE:(s + 1) * Q_TILE, :].T.astype(BF16)
        vwT_ref[s] = vw[s * Q_TILE:(s + 1) * Q_TILE, :].T.astype(BF16)

    gates = _sigmoid(_dot(hb, wg_ref[...]))
    for s in range(n_sub):
        gt = gates[s * Q_TILE:(s + 1) * Q_TILE, :].T
        gT_ref[s] = gt[0:2 * 16, :].reshape(KV_HEADS, 16, Q_TILE)

    zs_ref[...] = _dot(hb, ws_ref[...])


def _in_projection(x2, norm_g, sc, sh, w_in):
    l, d = x2.shape
    tm = ROW_TILE
    cuts = np.cumsum([0, ATTN_WIDTH] + [KV_WIDTH] * 6 + [3 * ATTN_HEADS])
    q_cols = np.arange(cuts[0], cuts[1])
    k_cols = np.concatenate([np.arange(cuts[j], cuts[j + 1]) for j in (1, 3, 5)])
    v_cols = np.concatenate([np.arange(cuts[j], cuts[j + 1]) for j in (2, 4, 6)])
    s_cols = np.arange(cuts[8], w_in.shape[1])
    ssm_w = len(s_cols)

    def rot(cols):
        c = cols.reshape(-1, 2, HEAD_DIM // 2)
        sign = np.concatenate([-np.ones_like(c[:, :1]), np.ones_like(c[:, :1])], axis=1)
        return c[:, ::-1].reshape(-1), sign.reshape(-1)

    g_cols = np.zeros((LANES,), np.int64)
    g_sign = np.zeros((LANES,), np.int64)
    per_kv = Q_PER_KV * 3
    for hh in range(KV_HEADS):
        g_cols[hh * 16:hh * 16 + per_kv] = cuts[7] + hh * per_kv + np.arange(per_kv)
        g_sign[hh * 16:hh * 16 + per_kv] = 1
    qr, qr_sign = rot(q_cols)
    kr, kr_sign = rot(k_cols)
    cols = np.concatenate([q_cols, qr, k_cols, kr, v_cols, g_cols, s_cols])
    sign = np.concatenate([np.ones_like(q_cols), qr_sign, np.ones_like(k_cols), kr_sign,
                           np.ones_like(v_cols), g_sign, np.ones_like(s_cols)])
    w_all = (jnp.take(w_in, jnp.asarray(cols, jnp.int32), axis=1)
             * jnp.asarray(sign, F32)[None, :]).astype(BF16)
    weights = [w_all]

    half = HEAD_DIM // 2
    inv = jnp.power(ROPE_THETA, -jnp.arange(half, dtype=F32) * 2.0 / HEAD_DIM)
    ang = jnp.arange(l, dtype=F32)[:, None] * inv[None, :]
    cos = jnp.tile(jnp.cos(ang), (1, LANES // half))
    sin = jnp.tile(jnp.sin(ang), (1, LANES // half))

    n_q = l // Q_TILE
    sub = tm // Q_TILE
    row = lambda i: (i, 0)
    const = lambda i: (0, 0)
    out_shape = [
        jax.ShapeDtypeStruct((n_q, KV_HEADS, Q_PER_KV * HEAD_DIM, Q_TILE), F32),
        jax.ShapeDtypeStruct((l, KV_WIDTH), F32),
        jax.ShapeDtypeStruct((l, KV_WIDTH), BF16),
        jax.ShapeDtypeStruct((l, KV_WIDTH), BF16),
        jax.ShapeDtypeStruct((l, KV_WIDTH), F32),
        jax.ShapeDtypeStruct((n_q, KV_WIDTH, Q_TILE), BF16),
        jax.ShapeDtypeStruct((n_q, KV_WIDTH, Q_TILE), BF16),
        jax.ShapeDtypeStruct((n_q, KV_HEADS, 16, Q_TILE), F32),
        jax.ShapeDtypeStruct((l, ssm_w), F32),
    ]
    out_specs = [
        pl.BlockSpec((sub, KV_HEADS, Q_PER_KV * HEAD_DIM, Q_TILE), lambda i: (i, 0, 0, 0)),
        pl.BlockSpec((tm, KV_WIDTH), row),
        pl.BlockSpec((tm, KV_WIDTH), row),
        pl.BlockSpec((tm, KV_WIDTH), row),
        pl.BlockSpec((tm, KV_WIDTH), row),
        pl.BlockSpec((sub, KV_WIDTH, Q_TILE), lambda i: (i, 0, 0)),
        pl.BlockSpec((sub, KV_WIDTH, Q_TILE), lambda i: (i, 0, 0)),
        pl.BlockSpec((sub, KV_HEADS, 16, Q_TILE), lambda i: (i, 0, 0, 0)),
        pl.BlockSpec((tm, ssm_w), row),
    ]
    in_specs = [pl.BlockSpec((tm, d), row),
                pl.BlockSpec((1, d), const), pl.BlockSpec((1, d), const), pl.BlockSpec((1, d), const),
                pl.BlockSpec((tm, LANES), row), pl.BlockSpec((tm, LANES), row)]
    in_specs += [pl.BlockSpec(w.shape, const) for w in weights]
    return pl.pallas_call(
        _inproj_kernel,
        grid=(l // tm,),
        in_specs=in_specs,
        out_specs=out_specs,
        out_shape=out_shape,
        compiler_params=_params(1),
        name="inproj",
    )(x2, norm_g.reshape(1, d), sc, sh, cos, sin, *weights)


def _compress_kernel(z_ref, pe_ref, w1_ref, w1p_ref, w2_ref, o_ref, *, transpose_out):
    n = z_ref.shape[0] // CMP_STRIDE
    first = jnp.zeros((n, w1_ref.shape[1]), F32)
    second = jnp.zeros((n, w1_ref.shape[1]), F32)
    for j in range(CMP_STRIDE):
        x_j = z_ref[pl.ds(j, n, stride=CMP_STRIDE), :].astype(BF16)
        first = first + _dot(x_j, w1p_ref[0, j])
        second = second + _dot(x_j, w1p_ref[0, CMP_STRIDE + j])
    bias = _dot(pe_ref[...].astype(BF16), w1_ref[...])[0:1]
    h1 = first + pltpu.roll(second, n - 1, 0) + bias
    out = _dot_hp(_gelu_tanh(h1), w2_ref[...])
    if transpose_out:
        o_ref[0] = out.T[0:HEAD_DIM, :]
    else:
        o_ref[0] = out[:, 0:HEAD_DIM]


def _compress(z, pe, w1, w2, transpose_out):
    l = z.shape[0]
    n = l // CMP_STRIDE
    hidden = w1.shape[1]
    assert CMP_BLOCK == 2 * CMP_STRIDE
    pe8 = jnp.broadcast_to(pe.reshape(1, CMP_BLOCK * HEAD_DIM), (8, CMP_BLOCK * HEAD_DIM))
    w1b = w1.astype(BF16)
    w1r = w1b.reshape(CMP_BLOCK, 1, HEAD_DIM, hidden)
    head_sel = jnp.eye(KV_HEADS, dtype=BF16)[:, None, :, None, None]
    w1p = (head_sel * w1r[None]).reshape(KV_HEADS, CMP_BLOCK, KV_WIDTH, hidden)
    w2p = jnp.pad(w2, ((0, 0), (0, LANES - HEAD_DIM)))
    oshape = (KV_HEADS, HEAD_DIM, n) if transpose_out else (KV_HEADS, n, HEAD_DIM)
    return pl.pallas_call(
        functools.partial(_compress_kernel, transpose_out=transpose_out),
        grid=(KV_HEADS,),
        in_specs=[pl.BlockSpec(z.shape, lambda h: (0, 0)),
                  pl.BlockSpec(pe8.shape, lambda h: (0, 0)),
                  pl.BlockSpec(w1b.shape, lambda h: (0, 0)),
                  pl.BlockSpec((1,) + w1p.shape[1:], lambda h: (h, 0, 0, 0)),
                  pl.BlockSpec(w2p.shape, lambda h: (0, 0))],
        out_specs=pl.BlockSpec((1,) + oshape[1:], lambda h: (h, 0, 0)),
        out_shape=jax.ShapeDtypeStruct(oshape, F32),
        compiler_params=_params(1),
        name="compress_v" if transpose_out else "compress_k",
    )(z, pe8, w1b, w1p, w2p)


def _tile4(a):
    return jnp.concatenate([a] * Q_PER_KV, axis=1)


MASKED = -1e30
P_PAD = 8


LOOP_STEPS = 2
SEL_STEP = 64
N_FORCED = 3


def _attn_kernel(qT_ref, kc_ref, vcT_ref, ks_ref, vsT_ref, *rest):
    kw_refs = rest[0:WIN_CHUNKS]
    vw_refs = rest[WIN_CHUNKS:2 * WIN_CHUNKS]
    gT_ref, o_ref, p_scr, bias_scr, own_scr, oc_scr = rest[2 * WIN_CHUNKS:2 * WIN_CHUNKS + 6]
    per_head = rest[2 * WIN_CHUNKS + 6:]
    H, G, DH, TQ = KV_HEADS, Q_PER_KV, HEAD_DIM, Q_TILE
    s_bufs = [per_head[k * H:(k + 1) * H] for k in range(LOOP_STEPS)]
    p_bufs = [per_head[(LOOP_STEPS + k) * H:(LOOP_STEPS + k + 1) * H] for k in range(LOOP_STEPS)]
    i = pl.program_id(0)
    n_sel = bias_scr.shape[0]
    W = G * TQ
    heads = range(H)

    q_all, q_pad = [], []
    for h in heads:
        qT = qT_ref[0, h]
        qa = jnp.concatenate([qT[g * DH:(g + 1) * DH, :] for g in range(G)], axis=1)
        zeros = jnp.zeros_like(qa)
        q_all.append(qa)
        q_pad.append(jnp.concatenate([qa if hh == h else zeros for hh in heads], axis=0).astype(BF16))
    v_rows = [slice(h * DH, (h + 1) * DH) for h in heads]

    t = i * TQ + jax.lax.broadcasted_iota(jnp.int32, (1, TQ), 1)
    cur = jax.lax.shift_right_logical(t, int(math.log2(SEL_BLOCK)))

    ratio = SEL_BLOCK // CMP_STRIDE

    def select(ns):
        nc = ns * ratio
        c_end = jax.lax.broadcasted_iota(jnp.int32, (nc, 1), 0) * CMP_STRIDE + (CMP_BLOCK - 1)
        c_bias = _tile4(jnp.where(c_end <= t, 0.0, MASKED))
        blk = jax.lax.broadcasted_iota(jnp.int32, (ns, 1), 0)
        blk_f = blk.astype(F32)
        forced = (blk == 0) | (blk == cur) | (blk == cur - 1)
        scores = []
        s_c = [_dot(kc_ref[h, 0:nc, :].astype(BF16), q_all[h].astype(BF16)) for h in heads]
        for h in heads:
            s = s_c[h] + c_bias
            m = jnp.max(s, axis=0, keepdims=True)
            e = jnp.exp2(s - m)
            any_valid = jnp.where(m > 0.5 * MASKED, 1.0, 0.0)
            p = e * (any_valid / jnp.sum(e, axis=0, keepdims=True))
            oc_scr[h] = _dot(vcT_ref[h, :, 0:nc].astype(BF16), p.astype(BF16))
            p_sum = p[:, 0:TQ]
            for g in range(1, G):
                p_sum = p_sum + p[:, g * TQ:(g + 1) * TQ]
            p_scr[h, 0:P_PAD, :] = jnp.zeros((P_PAD, TQ), F32)
            p_scr[h, P_PAD:P_PAD + nc, :] = p_sum

            def every(off):
                return p_scr[h, pl.ds(P_PAD + off, ns, stride=ratio), :]

            imp = 0.5 * (every(-1) + every(3)) + (every(0) + every(1) + every(2))
            scores.append(jnp.where((blk > cur) | forced, NEG_INF, imp))
        score = jnp.concatenate(scores, axis=1)

        def pick_one(_, score):
            m = jnp.max(score, axis=0, keepdims=True)
            first = jnp.min(jnp.where(score == m, blk_f, float(ns)), axis=0, keepdims=True)
            return jnp.where(blk_f == first, NEG_INF, score)

        left = jax.lax.fori_loop(0, SEL_TOPN - N_FORCED, pick_one, score)
        cur_w = jnp.concatenate([cur] * H, axis=1)
        bias = jnp.where((left == NEG_INF) & (blk <= cur_w), 0.0, MASKED)
        own_scr[0:1, :] = jnp.max(jnp.where(blk == 2 * i, bias, MASKED), axis=0, keepdims=True)
        own_scr[1:2, :] = jnp.max(jnp.where(blk == 2 * i + 1, bias, MASKED), axis=0, keepdims=True)
        bias_scr[0:ns, :] = jnp.where((blk == 2 * i) | (blk == 2 * i + 1), MASKED, bias)
        if ns < n_sel:
            bias_scr[ns:n_sel, :] = jnp.full((n_sel - ns, H * TQ), MASKED, F32)

    sizes = [min(n_sel, SEL_STEP * (k + 1)) for k in range(pl.cdiv(n_sel, SEL_STEP))]
    variant = jnp.minimum((2 * i + 1) // SEL_STEP, len(sizes) - 1)
    for k, ns in enumerate(sizes):
        pl.when(variant == k)(functools.partial(select, ns))

    KC = KEY_CHUNK
    BPC = KC // SEL_BLOCK
    VROWS = DH + 16
    onehot = (jax.lax.broadcasted_iota(jnp.int32, (KC, LANES), 1)
              == jax.lax.shift_right_logical(jax.lax.broadcasted_iota(jnp.int32, (KC, LANES), 0),
                                             int(math.log2(SEL_BLOCK)))).astype(BF16)
    ones_rows = jnp.ones((VROWS - DH, KC), BF16)
    pad_rows = jnp.zeros((LANES - BPC, W), F32)
    last_chunk = ks_ref.shape[0] // KC - 1

    n_main = (i + KC // TQ - 1) // (KC // TQ)

    def augmented_q(c):
        rows = bias_scr[pl.ds(pl.multiple_of(jnp.minimum(c, last_chunk) * BPC, BPC), BPC), :]
        rows = jnp.where(c < n_main, rows, MASKED)
        out = []
        for h in heads:
            extra = jnp.concatenate([_tile4(rows[:, h * TQ:(h + 1) * TQ]), pad_rows], axis=0)
            out.append(jnp.concatenate([q_pad[h], extra.astype(BF16)], axis=0))
        return out

    def scores(c):
        k = ks_ref[pl.ds(pl.multiple_of(jnp.minimum(c, last_chunk) * KC, KC), KC), :]
        k_aug = jnp.concatenate([k, onehot], axis=1)
        q_aug = augmented_q(c)
        out = []
        for h in heads:
            s = _dot(k_aug, q_aug[h])
            out.append((s, jnp.max(s, axis=0, keepdims=True)))
        return out

    def values(c):
        c = jnp.clip(c, 0, last_chunk)
        v = jnp.concatenate([vsT_ref[(KC // TQ) * c + j] for j in range(KC // TQ)], axis=1)
        return [jnp.concatenate([v[v_rows[h], :], ones_rows], axis=0) for h in heads]

    def chunk_step(c, m_run, acc, mx_cur, s_cur, s_next, p_prev, p_cur):
        v_prev = values(c - 1)
        nxt = scores(c + 1)
        m_new, mx_next = [], []
        for h in heads:
            s_next[h][...] = nxt[h][0]
            mx_next.append(nxt[h][1])
        for h in heads:
            acc[h] = acc[h] + _dot(v_prev[h], p_prev[h][...])
        for h in heads:
            m_new.append(jnp.maximum(m_run[h], mx_cur[h]))
            acc[h] = jnp.exp2(m_run[h] - m_new[h]) * acc[h]
            p_cur[h][...] = jnp.exp2(s_cur[h][...] - m_new[h]).astype(BF16)
        return m_new, acc, mx_next

    B = LOOP_STEPS

    def body(j, carry):
        m_run, acc, mx = (list(c) for c in carry)
        for k in range(B):
            m_run, acc, mx = chunk_step(B * j + k, m_run, acc, mx,
                                        s_bufs[k], s_bufs[(k + 1) % B], p_bufs[(k - 1) % B], p_bufs[k])
        return tuple(m_run), tuple(acc), tuple(mx)

    n_iter = (n_main + B - 1) // B
    s_0 = scores(0)
    for h in heads:
        s_bufs[0][h][...] = s_0[h][0]
        p_bufs[B - 1][h][...] = jnp.zeros(p_bufs[B - 1][h].shape, BF16)
    init = (tuple(jnp.full((1, W), MASKED, F32) for _ in heads),
            tuple(jnp.zeros((VROWS, W), F32) for _ in heads),
            tuple(s_0[h][1] for h in heads))
    m_run, acc, _ = jax.lax.fori_loop(0, n_iter, body, init)
    v_last = values(B * n_iter - 1)
    acc = tuple(acc[h] + _dot(v_last[h], p_bufs[B - 1][h][...]) for h in heads)

    row_o = jax.lax.broadcasted_iota(jnp.int32, (TQ, 1), 0)
    lane_o = jax.lax.broadcasted_iota(jnp.int32, (1, TQ), 1)
    k_own = ks_ref[pl.ds(pl.multiple_of(i * TQ, TQ), TQ), :]
    v_own = vsT_ref[i]
    k_w = jnp.concatenate([r[...] for r in kw_refs], axis=0)
    v_w = jnp.concatenate([r[0] for r in vw_refs], axis=1)
    w_pos = (i - (WIN_CHUNKS - 1)) * TQ + jax.lax.broadcasted_iota(jnp.int32, (WIN_CHUNKS * TQ, 1), 0)
    w_bias = _tile4(jnp.where((w_pos <= t) & (w_pos > t - WINDOW) & (w_pos >= 0), 0.0, MASKED))
    ones_w = jnp.ones((VROWS - DH, WIN_CHUNKS * TQ), BF16)

    s_own = [_dot(k_own, q_pad[h]) for h in heads]
    s_win = [_dot(k_w, q_pad[h]) for h in heads]
    head_out = []
    for h in heads:
        a = acc[h]
        own_lo = own_scr[0:1, h * TQ:(h + 1) * TQ]
        own_hi = own_scr[1:2, h * TQ:(h + 1) * TQ]
        own_bias = jnp.where(row_o <= lane_o, jnp.where(row_o < SEL_BLOCK, own_lo, own_hi), MASKED)
        s = s_own[h] + _tile4(own_bias)
        m_s = jnp.maximum(m_run[h], jnp.max(s, axis=0, keepdims=True))
        p = jnp.exp2(s - m_s).astype(BF16)
        a = jnp.exp2(m_run[h] - m_s) * a + _dot(jnp.concatenate([v_own[v_rows[h], :], ones_rows[:, 0:TQ]], axis=0), p)
        o_s = a[0:DH, :] / a[DH:DH + 1, :]

        s = s_win[h] + w_bias
        p = jnp.exp2(s - jnp.max(s, axis=0, keepdims=True)).astype(BF16)
        a_w = _dot(jnp.concatenate([v_w[v_rows[h], :], ones_w], axis=0), p)
        o_w = a_w[0:DH, :] / a_w[DH:DH + 1, :]

        o_c = oc_scr[h]
        gates = gT_ref[0, h]
        for g in range(G):
            cols = slice(g * TQ, (g + 1) * TQ)
            head_out.append(gates[3 * g:3 * g + 1, :] * o_c[:, cols]
                            + gates[3 * g + 1:3 * g + 2, :] * o_s[:, cols]
                            + gates[3 * g + 2:3 * g + 3, :] * o_w[:, cols])
    o_ref[...] = jnp.concatenate(head_out, axis=0).T


def _attention(qT, kc, vcT, ks, vsT, kw, vwT, gT, l):
    n_q = l // Q_TILE
    n_cmp = l // CMP_STRIDE
    n_sel = l // SEL_BLOCK
    assert CMP_BLOCK == 2 * CMP_STRIDE and SEL_BLOCK == 4 * CMP_STRIDE

    assert n_sel >= SEL_TOPN and KV_WIDTH == LANES
    H, W = KV_HEADS, Q_PER_KV * Q_TILE

    def win(j):
        return lambda i: (jnp.maximum(i - (WIN_CHUNKS - 1) + j, 0), 0)

    def win3(j):
        return lambda i: (jnp.maximum(i - (WIN_CHUNKS - 1) + j, 0), 0, 0)

    in_specs = [
        pl.BlockSpec((1, H, Q_PER_KV * HEAD_DIM, Q_TILE), lambda i: (i, 0, 0, 0)),
        pl.BlockSpec(kc.shape, lambda i: (0, 0, 0)),
        pl.BlockSpec(vcT.shape, lambda i: (0, 0, 0)),
        pl.BlockSpec(ks.shape, lambda i: (0, 0)),
        pl.BlockSpec(vsT.shape, lambda i: (0, 0, 0)),
    ]
    in_specs += [pl.BlockSpec((Q_TILE, KV_WIDTH), win(j)) for j in range(WIN_CHUNKS)]
    in_specs += [pl.BlockSpec((1, KV_WIDTH, Q_TILE), win3(j)) for j in range(WIN_CHUNKS)]
    in_specs += [pl.BlockSpec((1, H, 16, Q_TILE), lambda i: (i, 0, 0, 0))]
    return pl.pallas_call(
        _attn_kernel,
        grid=(n_q,),
        in_specs=in_specs,
        out_specs=pl.BlockSpec((Q_TILE, ATTN_WIDTH), lambda i: (i, 0)),
        out_shape=jax.ShapeDtypeStruct((l, ATTN_WIDTH), F32),
        scratch_shapes=[pltpu.VMEM((H, P_PAD + n_cmp, Q_TILE), F32),
                        pltpu.VMEM((n_sel, H * Q_TILE), F32),
                        pltpu.VMEM((8, H * Q_TILE), F32),
                        pltpu.VMEM((H, HEAD_DIM, W), F32)]
        + [pltpu.VMEM((KEY_CHUNK, W), F32)] * (LOOP_STEPS * H)
        + [pltpu.VMEM((KEY_CHUNK, W), BF16)] * (LOOP_STEPS * H),
        compiler_params=_params(1),
        name="attn",
    )(qT, kc, vcT, ks, vsT, *([kw] * WIN_CHUNKS), *([vwT] * WIN_CHUNKS), gT)


SSM_TILE = 256
SSM_PACK = 16


def _ssm_kernel(u_ref, a_ref, wb_ref, wc_ref, msk_ref, d_ref, wg_ref, bg_ref, y_ref, x_scr, st_scr):
    step = pl.program_id(0)
    tc, width = u_ref.shape
    P = SSM_PACK

    @pl.when(step == 0)
    def _():
        st_scr[...] = jnp.zeros_like(st_scr)

    PH = P // 2
    hw = width // 2
    msk = msk_ref[...]
    u = u_ref[...]

    for hf in range(2):
        u_half = u[:, hf * hw:(hf + 1) * hw]
        wb_half = wb_ref[hf * hw:(hf + 1) * hw, :]
        for j in range(PH):
            bu = _dot((u_half * msk[j:j + 1, :]).astype(BF16), wb_half)
            x_scr[hf, 0, pl.ds(j, tc, stride=PH), :] = bu[:, 0:LANES]
            x_scr[hf, 1, pl.ds(j, tc, stride=PH), :] = bu[:, LANES:2 * LANES]

    a_re = [a_ref[0, hf * PH:(hf + 1) * PH, :] for hf in range(2)]
    a_im = [a_ref[1, hf * PH:(hf + 1) * PH, :] for hf in range(2)]

    def one(ti, carry):
        rows = pl.ds(pl.multiple_of(ti * PH, PH), PH)
        out = []
        for hf in range(2):
            x_re, x_im = carry[hf]
            n_re = a_re[hf] * x_re - a_im[hf] * x_im + x_scr[hf, 0, rows, :]
            n_im = a_re[hf] * x_im + a_im[hf] * x_re + x_scr[hf, 1, rows, :]
            x_scr[hf, 0, rows, :] = n_re
            x_scr[hf, 1, rows, :] = n_im
            out.append((n_re, n_im))
        return tuple(out)

    init = tuple((st_scr[0, hf * PH:(hf + 1) * PH, :], st_scr[1, hf * PH:(hf + 1) * PH, :]) for hf in range(2))
    last = jax.lax.fori_loop(0, tc, one, init, unroll=8)
    for hf in range(2):
        st_scr[0, hf * PH:(hf + 1) * PH, :] = last[hf][0]
        st_scr[1, hf * PH:(hf + 1) * PH, :] = last[hf][1]

    halves = []
    for hf in range(2):
        wc_half = wc_ref[:, hf * hw:(hf + 1) * hw]
        y_half = jnp.zeros((tc, hw), F32)
        for j in range(PH):
            x_j = jnp.concatenate([x_scr[hf, 0, pl.ds(j, tc, stride=PH), :],
                                   x_scr[hf, 1, pl.ds(j, tc, stride=PH), :]], axis=1).astype(BF16)
            y_half = y_half + msk[j:j + 1, :] * _dot(x_j, wc_half)
        halves.append(y_half)
    y = _gelu_tanh(jnp.concatenate(halves, axis=1) + d_ref[...] * u)
    y_ref[...] = y * _sigmoid(_dot(y.astype(BF16), wg_ref[...]) + bg_ref[...])


def _s5(zs, lam_re, lam_im, log_step, b_re, b_im, c_re, c_im, d_skip, w_glu, b_glu):
    l, width = zs.shape
    G, Pn, C = b_re.shape
    assert Pn == SSM_STATE and C == SSM_GROUP and G * C == width and G == 2 * SSM_PACK
    step = jnp.exp(log_step)[:, None]
    mag = jnp.exp(lam_re * step)
    a_re = mag * jnp.cos(lam_im * step)
    a_im = mag * jnp.sin(lam_im * step)
    den = lam_re * lam_re + lam_im * lam_im
    n_re = a_re - 1.0
    f_re = (n_re * lam_re + a_im * lam_im) / den
    f_im = (a_im * lam_re - n_re * lam_im) / den
    bb_re = f_re[..., None] * b_re - f_im[..., None] * b_im
    bb_im = f_re[..., None] * b_im + f_im[..., None] * b_re
    a_pack = jnp.stack([a_re.reshape(SSM_PACK, 2 * Pn), a_im.reshape(SSM_PACK, 2 * Pn)])

    eye2 = jnp.eye(2, dtype=F32)

    def b_stack(bb):
        t = bb.reshape(SSM_PACK, 2, Pn, C)
        return jnp.einsum('jgpc,gh->jhcgp', t, eye2).reshape(width, 2 * Pn)

    w_b = jnp.concatenate([b_stack(bb_re), b_stack(bb_im)], axis=1).astype(BF16)

    def c_stack(cc):
        t = cc.reshape(SSM_PACK, 2, C, Pn)
        return jnp.einsum('jgcp,gh->gpjhc', t, eye2).reshape(2 * Pn, width)

    w_c = jnp.concatenate([c_stack(c_re), -c_stack(c_im)], axis=0).astype(BF16)
    half_rows, half_w = SSM_PACK // 2, width // 2
    lane_row = jnp.arange(half_w) // (2 * C)
    msk = (lane_row[None, :] == jnp.arange(half_rows)[:, None]).astype(F32)

    tc = SSM_TILE
    const2 = lambda i: (0, 0)
    return pl.pallas_call(
        _ssm_kernel,
        grid=(l // tc,),
        in_specs=[pl.BlockSpec((tc, width), lambda i: (i, 0)),
                  pl.BlockSpec(a_pack.shape, lambda i: (0, 0, 0)),
                  pl.BlockSpec(w_b.shape, const2),
                  pl.BlockSpec(w_c.shape, const2),
                  pl.BlockSpec(msk.shape, const2),
                  pl.BlockSpec((1, width), const2),
                  pl.BlockSpec((width, width), const2),
                  pl.BlockSpec((1, width), const2)],
        out_specs=pl.BlockSpec((tc, width), lambda i: (i, 0)),
        out_shape=jax.ShapeDtypeStruct((l, width), F32),
        scratch_shapes=[pltpu.VMEM((2, 2, tc * half_rows, LANES), F32),
                        pltpu.VMEM((2, SSM_PACK, LANES), F32)],
        compiler_params=_params(1),
        name="ssm",
    )(zs, a_pack, w_b, w_c, msk, d_skip.reshape(1, width), w_glu.astype(BF16), b_glu.reshape(1, width))


def _outproj_kernel(x_ref, oa_ref, ys_ref, ga_ref, gs_ref, wa_ref, ws_ref, gt_ref, o_ref):
    a = (_rms_scale(oa_ref[...]) * ga_ref[...]).astype(BF16)
    s = (_rms_scale(ys_ref[...]) * gs_ref[...]).astype(BF16)
    y = _dot(a, wa_ref[...]) + _dot(s, ws_ref[...])
    o_ref[...] = x_ref[...] + gt_ref[...] * y


def _out_projection(x2, o_attn, y_ssm, attn_g, ssm_g, w_o, gt):
    l, d = x2.shape
    tm = ROW_TILE
    aw, sw = o_attn.shape[1], y_ssm.shape[1]
    row = lambda i: (i, 0)
    const = lambda i: (0, 0)
    return pl.pallas_call(
        _outproj_kernel,
        grid=(l // tm,),
        in_specs=[pl.BlockSpec((tm, d), row), pl.BlockSpec((tm, aw), row), pl.BlockSpec((tm, sw), row),
                  pl.BlockSpec((1, aw), const), pl.BlockSpec((1, sw), const),
                  pl.BlockSpec((aw, d), const), pl.BlockSpec((sw, d), const), pl.BlockSpec((1, d), const)],
        out_specs=pl.BlockSpec((tm, d), row),
        out_shape=jax.ShapeDtypeStruct((l, d), F32),
        compiler_params=_params(1),
        name="outproj",
    )(x2, o_attn, y_ssm, attn_g.reshape(1, aw), ssm_g.reshape(1, sw),
      w_o[:aw].astype(BF16), w_o[aw:].astype(BF16), gt)


FFN_TILE = 256
FFN_COLS = 256
HALO = 8


def _ffn_kernel(x_ref, g_ref, sc_ref, sh_ref, wup_ref, cw_ref, cb_ref, wdn_ref, gt_ref, gf_ref,
                o_ref, ua_scr, uv_scr):
    step = pl.program_id(0)
    tm = x_ref.shape[0]
    n_col = wdn_ref.shape[0] // FFN_COLS

    @pl.when(step == 0)
    def _():
        ua_scr[...] = jnp.zeros_like(ua_scr)
        uv_scr[...] = jnp.zeros_like(uv_scr)

    x = x_ref[...]
    hb = (_rms_scale(x) * g_ref[...] * (1.0 + sc_ref[...]) + sh_ref[...]).astype(BF16)

    cw = FFN_COLS

    def up(scr, j, col):
        scr[j, 0:HALO, :] = scr[j, tm:tm + HALO, :]
        scr[j, HALO:tm + HALO, :] = _dot(hb, wup_ref[:, col * cw:(col + 1) * cw])

    def conv(scr, j, col):
        cols = slice(col * cw, (col + 1) * cw)
        w = cw_ref[:, cols]
        out = cb_ref[:, cols]
        for k in range(CONV_WIDTH):
            off = HALO - (CONV_WIDTH - 1) + k
            out = out + w[k:k + 1, :] * scr[j, off:off + tm, :]
        return out

    y = jnp.zeros(x.shape, F32)
    for j in range(n_col):
        up(ua_scr, j, j)
        up(uv_scr, j, n_col + j)
    for j in range(n_col):
        a = conv(ua_scr, j, j)
        v = conv(uv_scr, j, n_col + j)
        act = (a * _sigmoid(a) * v).astype(BF16)
        y = y + _dot(act, wdn_ref[j * cw:(j + 1) * cw, :])
    x2 = x + gt_ref[...] * y
    o_ref[...] = _rms_scale(x2) * gf_ref[...]


def _conv_ffn(x1, norm_g, sc, sh, w_up, conv_w, conv_b, w_down, gt, normf_g):
    l, d = x1.shape
    f = w_down.shape[0]
    tm, cw = FFN_TILE, FFN_COLS
    n_col = f // cw
    assert f % cw == 0
    wup = w_up.astype(BF16)
    cbt = conv_b.reshape(1, 2 * f)
    wdn = w_down.astype(BF16)
    row = lambda i: (i, 0)
    const = lambda i: (0, 0)
    return pl.pallas_call(
        _ffn_kernel,
        grid=(l // tm,),
        in_specs=[pl.BlockSpec((tm, d), row),
                  pl.BlockSpec((1, d), const), pl.BlockSpec((1, d), const), pl.BlockSpec((1, d), const),
                  pl.BlockSpec(wup.shape, const), pl.BlockSpec(conv_w.shape, const),
                  pl.BlockSpec(cbt.shape, const), pl.BlockSpec(wdn.shape, const),
                  pl.BlockSpec((1, d), const), pl.BlockSpec((1, d), const)],
        out_specs=pl.BlockSpec((tm, d), row),
        out_shape=jax.ShapeDtypeStruct((l, d), F32),
        scratch_shapes=[pltpu.VMEM((n_col, tm + HALO, cw), F32),
                        pltpu.VMEM((n_col, tm + HALO, cw), F32)],
        compiler_params=_params(1),
        name="ffn",
    )(x1, norm_g.reshape(1, d), sc, sh, wup, conv_w, cbt, wdn, gt, normf_g.reshape(1, d))


def kernel(x, c, norm1_g, norm2_g, normf_g, w_ada, b_ada, w_in, pe_k, pe_v, w_ck1, w_ck2, w_cv1, w_cv2,
           lam_re, lam_im, log_step, b_re, b_im, c_re, c_im, d_skip, w_glu, b_glu, attn_norm_g,
           ssm_norm_g, w_o, w_up, conv_w, conv_b, w_down):
    b, l, d = x.shape
    depth = w_ada.shape[0]
    assert b == 1 and depth == 1
    assert l % KEY_CHUNK == 0 and l % ROW_TILE == 0 and l >= WINDOW
    x2 = x.reshape(l, d)
    mod = _modulation(c, w_ada[0], b_ada[0])
    sh1, sc1, gt1, sh2, sc2, gt2 = [mod[:, k * d:(k + 1) * d] for k in range(6)]

    qT, k_c, k_s, k_w, v_c, v_sT, v_wT, gT, zs = _in_projection(x2, norm1_g[0], sc1, sh1, w_in[0])
    kc = _compress(k_c, pe_k[0], w_ck1[0], w_ck2[0], transpose_out=False)
    vcT = _compress(v_c, pe_v[0], w_cv1[0], w_cv2[0], transpose_out=True)
    o_attn = _attention(qT, kc, vcT, k_s, v_sT, k_w, v_wT, gT, l)
    y_ssm = _s5(zs, lam_re[0], lam_im[0], log_step[0], b_re[0], b_im[0], c_re[0], c_im[0],
                d_skip[0], w_glu[0], b_glu[0])
    x1 = _out_projection(x2, o_attn, y_ssm, attn_norm_g[0], ssm_norm_g[0], w_o[0], gt1)
    out = _conv_ffn(x1, norm2_g[0], sc2, sh2, w_up[0], conv_w[0], conv_b[0], w_down[0], gt2, normf_g)
    return out.reshape(b, l, d)
```

```python
import functools
import math

import jax
import jax.numpy as jnp
import numpy as np
from jax.experimental import pallas as pl
from jax.experimental.pallas import tpu as pltpu

ATTN_HEADS = 8
KV_HEADS = 2
HEAD_DIM = 64
Q_PER_KV = ATTN_HEADS // KV_HEADS
ATTN_WIDTH = ATTN_HEADS * HEAD_DIM
KV_WIDTH = KV_HEADS * HEAD_DIM
SSM_GROUP = 16
SSM_STATE = 64
CMP_BLOCK = 32
CMP_STRIDE = 16
SEL_BLOCK = 64
SEL_TOPN = 16
WINDOW = 512
FORCE_SCORE = 1.0e4
ROPE_THETA = 10000.0
CONV_WIDTH = 3
NORM_EPS = 1e-6

LANES = 128
Q_TILE = 128
KEY_CHUNK = 512
ROW_TILE = 512
WIN_CHUNKS = WINDOW // Q_TILE + 1
VMEM_LIMIT = 56 * 1024 * 1024

F32 = jnp.float32
BF16 = jnp.bfloat16
NEG_INF = float("-inf")


def _params(n_axes, vmem=VMEM_LIMIT):
    return pltpu.CompilerParams(dimension_semantics=("arbitrary",) * n_axes,
                                vmem_limit_bytes=vmem)


def _split(a):
    hi = a.astype(BF16)
    lo = (a - hi.astype(F32)).astype(BF16)
    return hi, lo


def _dot(a, b):
    return jnp.dot(a, b, preferred_element_type=F32)


def _dot_hp(a, b):
    a_hi, a_lo = _split(a)
    b_hi, b_lo = _split(b)
    return _dot(a_hi, b_hi) + (_dot(a_hi, b_lo) + _dot(a_lo, b_hi))


def _sigmoid(x):
    return 1.0 / (1.0 + jnp.exp(-x))


def _gelu_tanh(x):
    c = math.sqrt(2.0 / math.pi)
    return 0.5 * x * (1.0 + jnp.tanh(c * (x + 0.044715 * (x * x * x))))


def _rms_scale(x):
    return x * jax.lax.rsqrt(jnp.mean(x * x, axis=-1, keepdims=True) + NORM_EPS)


def _mod_kernel(c_ref, w_ref, b_ref, o_ref):
    c = c_ref[...]
    s = c * _sigmoid(c)
    o_ref[...] = _dot_hp(s, w_ref[...]) + b_ref[...]


def _modulation(c, w_ada, b_ada):
    d, n = w_ada.shape
    bn = n // 4
    c8 = jnp.broadcast_to(c, (8, d))
    out = pl.pallas_call(
        _mod_kernel,
        grid=(n // bn,),
        in_specs=[pl.BlockSpec((8, d), lambda j: (0, 0)),
                  pl.BlockSpec((d, bn), lambda j: (0, j)),
                  pl.BlockSpec((1, bn), lambda j: (0, j))],
        out_specs=pl.BlockSpec((8, bn), lambda j: (0, j)),
        out_shape=jax.ShapeDtypeStruct((8, n), F32),
        compiler_params=_params(1),
        name="mod",
    )(c8, w_ada, b_ada.reshape(1, n))
    return out[0:1]


def _inproj_kernel(x_ref, g_ref, sc_ref, sh_ref, cos_ref, sin_ref, w_ref,
                   qT_ref, kc_ref, ks_ref, kw_ref, vc_ref, vsT_ref, vwT_ref, gT_ref, zs_ref):
    x = x_ref[...]
    h = _rms_scale(x) * g_ref[...] * (1.0 + sc_ref[...]) + sh_ref[...]
    hb = h.astype(BF16)
    cos = cos_ref[...]
    sin = sin_ref[...]
    n_sub = x.shape[0] // Q_TILE
    widths = (ATTN_WIDTH, ATTN_WIDTH, 3 * KV_WIDTH, 3 * KV_WIDTH, 3 * KV_WIDTH, LANES, w_ref.shape[1])
    starts = np.concatenate([[0], np.cumsum(widths[:-1])])
    wq_ref, wqr_ref, wk_ref, wkr_ref, wv_ref, wg_ref, ws_ref = [
        w_ref.at[:, int(a):int(min(a + n, w_ref.shape[1]))] for a, n in zip(starts, widths)]

    zq = _dot(hb, wq_ref[...])
    zqr = _dot(hb, wqr_ref[...])
    scale = HEAD_DIM ** -0.5 * math.log2(math.e)
    q_cols = []
    for j in range(ATTN_WIDTH // LANES):
        sl = slice(j * LANES, (j + 1) * LANES)
        q_cols.append((zq[:, sl] * cos + zqr[:, sl] * sin) * scale)
    q = jnp.concatenate(q_cols, axis=1)
    for s in range(n_sub):
        qs = q[s * Q_TILE:(s + 1) * Q_TILE, :].T
        qT_ref[s] = qs.reshape(KV_HEADS, Q_PER_KV * HEAD_DIM, Q_TILE)

    zk = _dot(hb, wk_ref[...])
    zkr = _dot(hb, wkr_ref[...])
    k_outs = (kc_ref, ks_ref, kw_ref)
    for j in range(3):
        sl = slice(j * LANES, (j + 1) * LANES)
        kj = zk[:, sl] * cos + zkr[:, sl] * sin
        k_outs[j][...] = kj.astype(k_outs[j].dtype)

    zv = _dot(hb, wv_ref[...])
    vc_ref[...] = zv[:, 0:LANES]
    vs = zv[:, LANES:2 * LANES]
    vw = zv[:, 2 * LANES:3 * LANES]
    for s in range(n_sub):
        vsT_ref[s] = vs[s * Q_TILE:(s + 1) * Q_TILE, :].T.astype(BF16)
        vwT_ref[s] = vw[s * Q_TILE:(s + 1) * Q_TILE, :].T.astype(BF16)

    gates = _sigmoid(_dot(hb, wg_ref[...]))
    for s in range(n_sub):
        gt = gates[s * Q_TILE:(s + 1) * Q_TILE, :].T
        gT_ref[s] = gt[0:2 * 16, :].reshape(KV_HEADS, 16, Q_TILE)

    zs_ref[...] = _dot(hb, ws_ref[...])


def _in_projection(x2, norm_g, sc, sh, w_in):
    l, d = x2.shape
    tm = ROW_TILE
    cuts = np.cumsum([0, ATTN_WIDTH] + [KV_WIDTH] * 6 + [3 * ATTN_HEADS])
    q_cols = np.arange(cuts[0], cuts[1])
    k_cols = np.concatenate([np.arange(cuts[j], cuts[j + 1]) for j in (1, 3, 5)])
    v_cols = np.concatenate([np.arange(cuts[j], cuts[j + 1]) for j in (2, 4, 6)])
    s_cols = np.arange(cuts[8], w_in.shape[1])
    ssm_w = len(s_cols)

    def rot(cols):
        c = cols.reshape(-1, 2, HEAD_DIM // 2)
        sign = np.concatenate([-np.ones_like(c[:, :1]), np.ones_like(c[:, :1])], axis=1)
        return c[:, ::-1].reshape(-1), sign.reshape(-1)

    g_cols = np.zeros((LANES,), np.int64)
    g_sign = np.zeros((LANES,), np.int64)
    per_kv = Q_PER_KV * 3
    for hh in range(KV_HEADS):
        g_cols[hh * 16:hh * 16 + per_kv] = cuts[7] + hh * per_kv + np.arange(per_kv)
        g_sign[hh * 16:hh * 16 + per_kv] = 1
    qr, qr_sign = rot(q_cols)
    kr, kr_sign = rot(k_cols)
    cols = np.concatenate([q_cols, qr, k_cols, kr, v_cols, g_cols, s_cols])
    sign = np.concatenate([np.ones_like(q_cols), qr_sign, np.ones_like(k_cols), kr_sign,
                           np.ones_like(v_cols), g_sign, np.ones_like(s_cols)])
    w_all = (jnp.take(w_in, jnp.asarray(cols, jnp.int32), axis=1)
             * jnp.asarray(sign, F32)[None, :]).astype(BF16)
    weights = [w_all]

    half = HEAD_DIM // 2
    inv = jnp.power(ROPE_THETA, -jnp.arange(half, dtype=F32) * 2.0 / HEAD_DIM)
    ang = jnp.arange(l, dtype=F32)[:, None] * inv[None, :]
    cos = jnp.tile(jnp.cos(ang), (1, LANES // half))
    sin = jnp.tile(jnp.sin(ang), (1, LANES // half))

    n_q = l // Q_TILE
    sub = tm // Q_TILE
    row = lambda i: (i, 0)
    const = lambda i: (0, 0)
    out_shape = [
        jax.ShapeDtypeStruct((n_q, KV_HEADS, Q_PER_KV * HEAD_DIM, Q_TILE), F32),
        jax.ShapeDtypeStruct((l, KV_WIDTH), F32),
        jax.ShapeDtypeStruct((l, KV_WIDTH), BF16),
        jax.ShapeDtypeStruct((l, KV_WIDTH), BF16),
        jax.ShapeDtypeStruct((l, KV_WIDTH), F32),
        jax.ShapeDtypeStruct((n_q, KV_WIDTH, Q_TILE), BF16),
        jax.ShapeDtypeStruct((n_q, KV_WIDTH, Q_TILE), BF16),
        jax.ShapeDtypeStruct((n_q, KV_HEADS, 16, Q_TILE), F32),
        jax.ShapeDtypeStruct((l, ssm_w), F32),
    ]
    out_specs = [
        pl.BlockSpec((sub, KV_HEADS, Q_PER_KV * HEAD_DIM, Q_TILE), lambda i: (i, 0, 0, 0)),
        pl.BlockSpec((tm, KV_WIDTH), row),
        pl.BlockSpec((tm, KV_WIDTH), row),
        pl.BlockSpec((tm, KV_WIDTH), row),
        pl.BlockSpec((tm, KV_WIDTH), row),
        pl.BlockSpec((sub, KV_WIDTH, Q_TILE), lambda i: (i, 0, 0)),
        pl.BlockSpec((sub, KV_WIDTH, Q_TILE), lambda i: (i, 0, 0)),
        pl.BlockSpec((sub, KV_HEADS, 16, Q_TILE), lambda i: (i, 0, 0, 0)),
        pl.BlockSpec((tm, ssm_w), row),
    ]
    in_specs = [pl.BlockSpec((tm, d), row),
                pl.BlockSpec((1, d), const), pl.BlockSpec((1, d), const), pl.BlockSpec((1, d), const),
                pl.BlockSpec((tm, LANES), row), pl.BlockSpec((tm, LANES), row)]
    in_specs += [pl.BlockSpec(w.shape, const) for w in weights]
    return pl.pallas_call(
        _inproj_kernel,
        grid=(l // tm,),
        in_specs=in_specs,
        out_specs=out_specs,
        out_shape=out_shape,
        compiler_params=_params(1),
        name="inproj",
    )(x2, norm_g.reshape(1, d), sc, sh, cos, sin, *weights)


def _compress_kernel(z_ref, pe_ref, w1_ref, w1p_ref, w2_ref, o_ref, *, transpose_out):
    n = z_ref.shape[0] // CMP_STRIDE
    first = jnp.zeros((n, w1_ref.shape[1]), F32)
    second = jnp.zeros((n, w1_ref.shape[1]), F32)
    for j in range(CMP_STRIDE):
        x_j = z_ref[pl.ds(j, n, stride=CMP_STRIDE), :].astype(BF16)
        first = first + _dot(x_j, w1p_ref[0, j])
        second = second + _dot(x_j, w1p_ref[0, CMP_STRIDE + j])
    bias = _dot(pe_ref[...].astype(BF16), w1_ref[...])[0:1]
    h1 = first + pltpu.roll(second, n - 1, 0) + bias
    out = _dot_hp(_gelu_tanh(h1), w2_ref[...])
    if transpose_out:
        o_ref[0] = out.T[0:HEAD_DIM, :]
    else:
        o_ref[0] = out[:, 0:HEAD_DIM]


def _compress(z, pe, w1, w2, transpose_out):
    l = z.shape[0]
    n = l // CMP_STRIDE
    hidden = w1.shape[1]
    assert CMP_BLOCK == 2 * CMP_STRIDE
    pe8 = jnp.broadcast_to(pe.reshape(1, CMP_BLOCK * HEAD_DIM), (8, CMP_BLOCK * HEAD_DIM))
    w1b = w1.astype(BF16)
    w1r = w1b.reshape(CMP_BLOCK, 1, HEAD_DIM, hidden)
    head_sel = jnp.eye(KV_HEADS, dtype=BF16)[:, None, :, None, None]
    w1p = (head_sel * w1r[None]).reshape(KV_HEADS, CMP_BLOCK, KV_WIDTH, hidden)
    w2p = jnp.pad(w2, ((0, 0), (0, LANES - HEAD_DIM)))
    oshape = (KV_HEADS, HEAD_DIM, n) if transpose_out else (KV_HEADS, n, HEAD_DIM)
    return pl.pallas_call(
        functools.partial(_compress_kernel, transpose_out=transpose_out),
        grid=(KV_HEADS,),
        in_specs=[pl.BlockSpec(z.shape, lambda h: (0, 0)),
                  pl.BlockSpec(pe8.shape, lambda h: (0, 0)),
                  pl.BlockSpec(w1b.shape, lambda h: (0, 0)),
                  pl.BlockSpec((1,) + w1p.shape[1:], lambda h: (h, 0, 0, 0)),
                  pl.BlockSpec(w2p.shape, lambda h: (0, 0))],
        out_specs=pl.BlockSpec((1,) + oshape[1:], lambda h: (h, 0, 0)),
        out_shape=jax.ShapeDtypeStruct(oshape, F32),
        compiler_params=_params(1),
        name="compress_v" if transpose_out else "compress_k",
    )(z, pe8, w1b, w1p, w2p)


def _tile4(a):
    return jnp.concatenate([a] * Q_PER_KV, axis=1)


MASKED = -1e30
P_PAD = 8


LOOP_STEPS = 2
SEL_STEP = 64
N_FORCED = 3


def _attn_kernel(qT_ref, kc_ref, vcT_ref, ks_ref, vsT_ref, *rest):
    kw_refs = rest[0:WIN_CHUNKS]
    vw_refs = rest[WIN_CHUNKS:2 * WIN_CHUNKS]
    gT_ref, o_ref, p_scr, bias_scr, own_scr, oc_scr, ow_scr, stage_scr = rest[2 * WIN_CHUNKS:2 * WIN_CHUNKS + 8]
    per_head = rest[2 * WIN_CHUNKS + 8:]
    H, G, DH, TQ = KV_HEADS, Q_PER_KV, HEAD_DIM, Q_TILE
    s_bufs = [per_head[k * H:(k + 1) * H] for k in range(LOOP_STEPS)]
    p_bufs = [per_head[(LOOP_STEPS + k) * H:(LOOP_STEPS + k + 1) * H] for k in range(LOOP_STEPS)]
    i = pl.program_id(0)
    n_sel = bias_scr.shape[0]
    W = G * TQ
    heads = range(H)

    q_all, q_pad = [], []
    for h in heads:
        qT = qT_ref[0, h]
        qa = jnp.concatenate([qT[g * DH:(g + 1) * DH, :] for g in range(G)], axis=1)
        zeros = jnp.zeros_like(qa)
        q_all.append(qa)
        q_pad.append(jnp.concatenate([qa if hh == h else zeros for hh in heads], axis=0).astype(BF16))
    v_rows = [slice(h * DH, (h + 1) * DH) for h in heads]

    t = i * TQ + jax.lax.broadcasted_iota(jnp.int32, (1, TQ), 1)
    cur = jax.lax.shift_right_logical(t, int(math.log2(SEL_BLOCK)))

    ratio = SEL_BLOCK // CMP_STRIDE

    def select(ns):
        nc = ns * ratio
        c_end = jax.lax.broadcasted_iota(jnp.int32, (nc, 1), 0) * CMP_STRIDE + (CMP_BLOCK - 1)
        c_bias = _tile4(jnp.where(c_end <= t, 0.0, MASKED))
        blk = jax.lax.broadcasted_iota(jnp.int32, (ns, 1), 0)
        blk_f = blk.astype(F32)
        forced = (blk == 0) | (blk == cur) | (blk == cur - 1)
        scores = []
        for h in heads:
            stage_scr[h, 0:nc, :] = _dot(kc_ref[h, 0:nc, :].astype(BF16), q_all[h].astype(BF16))
        for h in heads:
            s = stage_scr[h, 0:nc, :] + c_bias
            m = jnp.max(s, axis=0, keepdims=True)
            e = jnp.exp2(s - m)
            any_valid = jnp.where(m > 0.5 * MASKED, 1.0, 0.0)
            p = e * (any_valid / jnp.sum(e, axis=0, keepdims=True))
            oc_scr[h] = _dot(vcT_ref[h, :, 0:nc].astype(BF16), p.astype(BF16))
            p_sum = p[:, 0:TQ]
            for g in range(1, G):
                p_sum = p_sum + p[:, g * TQ:(g + 1) * TQ]
            p_scr[h, 0:P_PAD, :] = jnp.zeros((P_PAD, TQ), F32)
            p_scr[h, P_PAD:P_PAD + nc, :] = p_sum

            def every(off):
                return p_scr[h, pl.ds(P_PAD + off, ns, stride=ratio), :]

            imp = 0.5 * (every(-1) + every(3)) + (every(0) + every(1) + every(2))
            scores.append(jnp.where((blk > cur) | forced, NEG_INF, imp))
        score = jnp.concatenate(scores, axis=1)

        def pick_one(_, score):
            m = jnp.max(score, axis=0, keepdims=True)
            first = jnp.min(jnp.where(score == m, blk_f, float(ns)), axis=0, keepdims=True)
            return jnp.where(blk_f == first, NEG_INF, score)

        left = jax.lax.fori_loop(0, SEL_TOPN - N_FORCED, pick_one, score)
        cur_w = jnp.concatenate([cur] * H, axis=1)
        bias = jnp.where((left == NEG_INF) & (blk <= cur_w), 0.0, MASKED)
        own_scr[0:1, :] = jnp.max(jnp.where(blk == 2 * i, bias, MASKED), axis=0, keepdims=True)
        own_scr[1:2, :] = jnp.max(jnp.where(blk == 2 * i + 1, bias, MASKED), axis=0, keepdims=True)
        bias_scr[0:ns, :] = jnp.where((blk == 2 * i) | (blk == 2 * i + 1), MASKED, bias)
        if ns < n_sel:
            bias_scr[ns:n_sel, :] = jnp.full((n_sel - ns, H * TQ), MASKED, F32)

    sizes = [min(n_sel, SEL_STEP * (k + 1)) for k in range(pl.cdiv(n_sel, SEL_STEP))]
    variant = jnp.minimum((2 * i + 1) // SEL_STEP, len(sizes) - 1)
    for k, ns in enumerate(sizes):
        pl.when(variant == k)(functools.partial(select, ns))

    KC = KEY_CHUNK
    BPC = KC // SEL_BLOCK
    VROWS = DH + 16
    onehot = (jax.lax.broadcasted_iota(jnp.int32, (KC, LANES), 1)
              == jax.lax.shift_right_logical(jax.lax.broadcasted_iota(jnp.int32, (KC, LANES), 0),
                                             int(math.log2(SEL_BLOCK)))).astype(BF16)
    ones_rows = jnp.ones((VROWS - DH, KC), BF16)
    pad_rows = jnp.zeros((LANES - BPC, W), F32)
    last_chunk = ks_ref.shape[0] // KC - 1

    n_main = (i + KC // TQ - 1) // (KC // TQ)

    def augmented_q(c):
        rows = bias_scr[pl.ds(pl.multiple_of(jnp.minimum(c, last_chunk) * BPC, BPC), BPC), :]
        rows = jnp.where(c < n_main, rows, MASKED)
        out = []
        for h in heads:
            extra = jnp.concatenate([_tile4(rows[:, h * TQ:(h + 1) * TQ]), pad_rows], axis=0)
            out.append(jnp.concatenate([q_pad[h], extra.astype(BF16)], axis=0))
        return out

    def scores(c):
        k = ks_ref[pl.ds(pl.multiple_of(jnp.minimum(c, last_chunk) * KC, KC), KC), :]
        k_aug = jnp.concatenate([k, onehot], axis=1)
        q_aug = augmented_q(c)
        out = []
        for h in heads:
            s = _dot(k_aug, q_aug[h])
            out.append((s, jnp.max(s, axis=0, keepdims=True)))
        return out

    def values(c):
        c = jnp.clip(c, 0, last_chunk)
        v = jnp.concatenate([vsT_ref[(KC // TQ) * c + j] for j in range(KC // TQ)], axis=1)
        return [jnp.concatenate([v[v_rows[h], :], ones_rows], axis=0) for h in heads]

    def chunk_step(c, m_run, acc, mx_cur, s_cur, s_next, p_prev, p_cur):
        v_prev = values(c - 1)
        nxt = scores(c + 1)
        m_new, mx_next = [], []
        for h in heads:
            s_next[h][...] = nxt[h][0]
            mx_next.append(nxt[h][1])
        for h in heads:
            acc[h] = acc[h] + _dot(v_prev[h], p_prev[h][...])
        for h in heads:
            m_new.append(jnp.maximum(m_run[h], mx_cur[h]))
            acc[h] = jnp.exp2(m_run[h] - m_new[h]) * acc[h]
            p_cur[h][...] = jnp.exp2(s_cur[h][...] - m_new[h]).astype(BF16)
        return m_new, acc, mx_next

    B = LOOP_STEPS

    def body(j, carry):
        m_run, acc, mx = (list(c) for c in carry)
        for k in range(B):
            m_run, acc, mx = chunk_step(B * j + k, m_run, acc, mx,
                                        s_bufs[k], s_bufs[(k + 1) % B], p_bufs[(k - 1) % B], p_bufs[k])
        return tuple(m_run), tuple(acc), tuple(mx)

    n_iter = (n_main + B - 1) // B

    k_own = ks_ref[pl.ds(pl.multiple_of(i * TQ, TQ), TQ), :]
    v_own = vsT_ref[i]
    k_w = jnp.concatenate([r[...] for r in kw_refs], axis=0)
    v_w = jnp.concatenate([r[0] for r in vw_refs], axis=1)
    n_win = WIN_CHUNKS * TQ
    w_pos = (i - (WIN_CHUNKS - 1)) * TQ + jax.lax.broadcasted_iota(jnp.int32, (n_win, 1), 0)
    w_bias = _tile4(jnp.where((w_pos <= t) & (w_pos > t - WINDOW) & (w_pos >= 0), 0.0, MASKED))
    ones_w = jnp.ones((VROWS - DH, n_win), BF16)
    for h in heads:
        stage_scr[h, 0:TQ, :] = _dot(k_own, q_pad[h])
        stage_scr[h, TQ:TQ + n_win, :] = _dot(k_w, q_pad[h])
    s_0 = scores(0)
    for h in heads:
        s_bufs[0][h][...] = s_0[h][0]
        p_bufs[B - 1][h][...] = jnp.zeros(p_bufs[B - 1][h].shape, BF16)
    for h in heads:
        s = stage_scr[h, TQ:TQ + n_win, :] + w_bias
        p = jnp.exp2(s - jnp.max(s, axis=0, keepdims=True)).astype(BF16)
        a_w = _dot(jnp.concatenate([v_w[v_rows[h], :], ones_w], axis=0), p)
        ow_scr[h] = a_w[0:DH, :] / a_w[DH:DH + 1, :]

    init = (tuple(jnp.full((1, W), MASKED, F32) for _ in heads),
            tuple(jnp.zeros((VROWS, W), F32) for _ in heads),
            tuple(s_0[h][1] for h in heads))
    m_run, acc, _ = jax.lax.fori_loop(0, n_iter, body, init)
    v_last = values(B * n_iter - 1)
    acc = tuple(acc[h] + _dot(v_last[h], p_bufs[B - 1][h][...]) for h in heads)

    row_o = jax.lax.broadcasted_iota(jnp.int32, (TQ, 1), 0)
    lane_o = jax.lax.broadcasted_iota(jnp.int32, (1, TQ), 1)
    head_out = []
    for h in heads:
        a = acc[h]
        own_lo = own_scr[0:1, h * TQ:(h + 1) * TQ]
        own_hi = own_scr[1:2, h * TQ:(h + 1) * TQ]
        own_bias = jnp.where(row_o <= lane_o, jnp.where(row_o < SEL_BLOCK, own_lo, own_hi), MASKED)
        s = stage_scr[h, 0:TQ, :] + _tile4(own_bias)
        m_s = jnp.maximum(m_run[h], jnp.max(s, axis=0, keepdims=True))
        p = jnp.exp2(s - m_s).astype(BF16)
        a = jnp.exp2(m_run[h] - m_s) * a + _dot(jnp.concatenate([v_own[v_rows[h], :], ones_rows[:, 0:TQ]], axis=0), p)
        o_s = a[0:DH, :] / a[DH:DH + 1, :]

        o_w = ow_scr[h]
        o_c = oc_scr[h]
        gates = gT_ref[0, h]
        for g in range(G):
            cols = slice(g * TQ, (g + 1) * TQ)
            head_out.append(gates[3 * g:3 * g + 1, :] * o_c[:, cols]
                            + gates[3 * g + 1:3 * g + 2, :] * o_s[:, cols]
                            + gates[3 * g + 2:3 * g + 3, :] * o_w[:, cols])
    o_ref[...] = jnp.concatenate(head_out, axis=0).T


def _attention(qT, kc, vcT, ks, vsT, kw, vwT, gT, l):
    n_q = l // Q_TILE
    n_cmp = l // CMP_STRIDE
    n_sel = l // SEL_BLOCK
    assert CMP_BLOCK == 2 * CMP_STRIDE and SEL_BLOCK == 4 * CMP_STRIDE

    assert n_sel >= SEL_TOPN and KV_WIDTH == LANES
    H, W = KV_HEADS, Q_PER_KV * Q_TILE

    def win(j):
        return lambda i: (jnp.maximum(i - (WIN_CHUNKS - 1) + j, 0), 0)

    def win3(j):
        return lambda i: (jnp.maximum(i - (WIN_CHUNKS - 1) + j, 0), 0, 0)

    in_specs = [
        pl.BlockSpec((1, H, Q_PER_KV * HEAD_DIM, Q_TILE), lambda i: (i, 0, 0, 0)),
        pl.BlockSpec(kc.shape, lambda i: (0, 0, 0)),
        pl.BlockSpec(vcT.shape, lambda i: (0, 0, 0)),
        pl.BlockSpec(ks.shape, lambda i: (0, 0)),
        pl.BlockSpec(vsT.shape, lambda i: (0, 0, 0)),
    ]
    in_specs += [pl.BlockSpec((Q_TILE, KV_WIDTH), win(j)) for j in range(WIN_CHUNKS)]
    in_specs += [pl.BlockSpec((1, KV_WIDTH, Q_TILE), win3(j)) for j in range(WIN_CHUNKS)]
    in_specs += [pl.BlockSpec((1, H, 16, Q_TILE), lambda i: (i, 0, 0, 0))]
    return pl.pallas_call(
        _attn_kernel,
        grid=(n_q,),
        in_specs=in_specs,
        out_specs=pl.BlockSpec((Q_TILE, ATTN_WIDTH), lambda i: (i, 0)),
        out_shape=jax.ShapeDtypeStruct((l, ATTN_WIDTH), F32),
        scratch_shapes=[pltpu.VMEM((H, P_PAD + n_cmp, Q_TILE), F32),
                        pltpu.VMEM((n_sel, H * Q_TILE), F32),
                        pltpu.VMEM((8, H * Q_TILE), F32),
                        pltpu.VMEM((H, HEAD_DIM, W), F32),
                        pltpu.VMEM((H, HEAD_DIM, W), F32),
                        pltpu.VMEM((H, max(n_cmp, (WIN_CHUNKS + 1) * Q_TILE), W), F32)]
        + [pltpu.VMEM((KEY_CHUNK, W), F32)] * (LOOP_STEPS * H)
        + [pltpu.VMEM((KEY_CHUNK, W), BF16)] * (LOOP_STEPS * H),
        compiler_params=_params(1),
        name="attn",
    )(qT, kc, vcT, ks, vsT, *([kw] * WIN_CHUNKS), *([vwT] * WIN_CHUNKS), gT)


SSM_TILE = 256
SSM_PACK = 16


def _ssm_kernel(u_ref, a_ref, wb_ref, wc_ref, msk_ref, d_ref, wg_ref, bg_ref, y_ref, x_scr, st_scr):
    step = pl.program_id(0)
    tc, width = u_ref.shape
    P = SSM_PACK

    @pl.when(step == 0)
    def _():
        st_scr[...] = jnp.zeros_like(st_scr)

    PH = P // 2
    hw = width // 2
    msk = msk_ref[...]
    u = u_ref[...]

    for hf in range(2):
        u_half = u[:, hf * hw:(hf + 1) * hw]
        wb_half = wb_ref[hf * hw:(hf + 1) * hw, :]
        for j in range(PH):
            bu = _dot((u_half * msk[j:j + 1, :]).astype(BF16), wb_half)
            x_scr[hf, 0, pl.ds(j, tc, stride=PH), :] = bu[:, 0:LANES]
            x_scr[hf, 1, pl.ds(j, tc, stride=PH), :] = bu[:, LANES:2 * LANES]

    a_re = [a_ref[0, hf * PH:(hf + 1) * PH, :] for hf in range(2)]
    a_im = [a_ref[1, hf * PH:(hf + 1) * PH, :] for hf in range(2)]

    def one(ti, carry):
        rows = pl.ds(pl.multiple_of(ti * PH, PH), PH)
        out = []
        for hf in range(2):
            x_re, x_im = carry[hf]
            n_re = a_re[hf] * x_re - a_im[hf] * x_im + x_scr[hf, 0, rows, :]
            n_im = a_re[hf] * x_im + a_im[hf] * x_re + x_scr[hf, 1, rows, :]
            x_scr[hf, 0, rows, :] = n_re
            x_scr[hf, 1, rows, :] = n_im
            out.append((n_re, n_im))
        return tuple(out)

    init = tuple((st_scr[0, hf * PH:(hf + 1) * PH, :], st_scr[1, hf * PH:(hf + 1) * PH, :]) for hf in range(2))
    last = jax.lax.fori_loop(0, tc, one, init, unroll=8)
    for hf in range(2):
        st_scr[0, hf * PH:(hf + 1) * PH, :] = last[hf][0]
        st_scr[1, hf * PH:(hf + 1) * PH, :] = last[hf][1]

    halves = []
    for hf in range(2):
        wc_half = wc_ref[:, hf * hw:(hf + 1) * hw]
        y_half = jnp.zeros((tc, hw), F32)
        for j in range(PH):
            x_j = jnp.concatenate([x_scr[hf, 0, pl.ds(j, tc, stride=PH), :],
                                   x_scr[hf, 1, pl.ds(j, tc, stride=PH), :]], axis=1).astype(BF16)
            y_half = y_half + msk[j:j + 1, :] * _dot(x_j, wc_half)
        halves.append(y_half)
    y = _gelu_tanh(jnp.concatenate(halves, axis=1) + d_ref[...] * u)
    y_ref[...] = y * _sigmoid(_dot(y.astype(BF16), wg_ref[...]) + bg_ref[...])


def _s5(zs, lam_re, lam_im, log_step, b_re, b_im, c_re, c_im, d_skip, w_glu, b_glu):
    l, width = zs.shape
    G, Pn, C = b_re.shape
    assert Pn == SSM_STATE and C == SSM_GROUP and G * C == width and G == 2 * SSM_PACK
    step = jnp.exp(log_step)[:, None]
    mag = jnp.exp(lam_re * step)
    a_re = mag * jnp.cos(lam_im * step)
    a_im = mag * jnp.sin(lam_im * step)
    den = lam_re * lam_re + lam_im * lam_im
    n_re = a_re - 1.0
    f_re = (n_re * lam_re + a_im * lam_im) / den
    f_im = (a_im * lam_re - n_re * lam_im) / den
    bb_re = f_re[..., None] * b_re - f_im[..., None] * b_im
    bb_im = f_re[..., None] * b_im + f_im[..., None] * b_re
    a_pack = jnp.stack([a_re.reshape(SSM_PACK, 2 * Pn), a_im.reshape(SSM_PACK, 2 * Pn)])

    eye2 = jnp.eye(2, dtype=F32)

    def b_stack(bb):
        t = bb.reshape(SSM_PACK, 2, Pn, C)
        return jnp.einsum('jgpc,gh->jhcgp', t, eye2).reshape(width, 2 * Pn)

    w_b = jnp.concatenate([b_stack(bb_re), b_stack(bb_im)], axis=1).astype(BF16)

    def c_stack(cc):
        t = cc.reshape(SSM_PACK, 2, C, Pn)
        return jnp.einsum('jgcp,gh->gpjhc', t, eye2).reshape(2 * Pn, width)

    w_c = jnp.concatenate([c_stack(c_re), -c_stack(c_im)], axis=0).astype(BF16)
    half_rows, half_w = SSM_PACK // 2, width // 2
    lane_row = jnp.arange(half_w) // (2 * C)
    msk = (lane_row[None, :] == jnp.arange(half_rows)[:, None]).astype(F32)

    tc = SSM_TILE
    const2 = lambda i: (0, 0)
    return pl.pallas_call(
        _ssm_kernel,
        grid=(l // tc,),
        in_specs=[pl.BlockSpec((tc, width), lambda i: (i, 0)),
                  pl.BlockSpec(a_pack.shape, lambda i: (0, 0, 0)),
                  pl.BlockSpec(w_b.shape, const2),
                  pl.BlockSpec(w_c.shape, const2),
                  pl.BlockSpec(msk.shape, const2),
                  pl.BlockSpec((1, width), const2),
                  pl.BlockSpec((width, width), const2),
                  pl.BlockSpec((1, width), const2)],
        out_specs=pl.BlockSpec((tc, width), lambda i: (i, 0)),
        out_shape=jax.ShapeDtypeStruct((l, width), F32),
        scratch_shapes=[pltpu.VMEM((2, 2, tc * half_rows, LANES), F32),
                        pltpu.VMEM((2, SSM_PACK, LANES), F32)],
        compiler_params=_params(1),
        name="ssm",
    )(zs, a_pack, w_b, w_c, msk, d_skip.reshape(1, width), w_glu.astype(BF16), b_glu.reshape(1, width))


def _outproj_kernel(x_ref, oa_ref, ys_ref, ga_ref, gs_ref, wa_ref, ws_ref, gt_ref, o_ref):
    a = (_rms_scale(oa_ref[...]) * ga_ref[...]).astype(BF16)
    s = (_rms_scale(ys_ref[...]) * gs_ref[...]).astype(BF16)
    y = _dot(a, wa_ref[...]) + _dot(s, ws_ref[...])
    o_ref[...] = x_ref[...] + gt_ref[...] * y


def _out_projection(x2, o_attn, y_ssm, attn_g, ssm_g, w_o, gt):
    l, d = x2.shape
    tm = ROW_TILE
    aw, sw = o_attn.shape[1], y_ssm.shape[1]
    row = lambda i: (i, 0)
    const = lambda i: (0, 0)
    return pl.pallas_call(
        _outproj_kernel,
        grid=(l // tm,),
        in_specs=[pl.BlockSpec((tm, d), row), pl.BlockSpec((tm, aw), row), pl.BlockSpec((tm, sw), row),
                  pl.BlockSpec((1, aw), const), pl.BlockSpec((1, sw), const),
                  pl.BlockSpec((aw, d), const), pl.BlockSpec((sw, d), const), pl.BlockSpec((1, d), const)],
        out_specs=pl.BlockSpec((tm, d), row),
        out_shape=jax.ShapeDtypeStruct((l, d), F32),
        compiler_params=_params(1),
        name="outproj",
    )(x2, o_attn, y_ssm, attn_g.reshape(1, aw), ssm_g.reshape(1, sw),
      w_o[:aw].astype(BF16), w_o[aw:].astype(BF16), gt)


FFN_TILE = 256
FFN_COLS = 256
HALO = 8


def _ffn_kernel(x_ref, g_ref, sc_ref, sh_ref, wup_ref, cw_ref, cb_ref, wdn_ref, gt_ref, gf_ref,
                o_ref, ua_scr, uv_scr):
    step = pl.program_id(0)
    tm = x_ref.shape[0]
    n_col = wdn_ref.shape[0] // FFN_COLS

    @pl.when(step == 0)
    def _():
        ua_scr[...] = jnp.zeros_like(ua_scr)
        uv_scr[...] = jnp.zeros_like(uv_scr)

    x = x_ref[...]
    hb = (_rms_scale(x) * g_ref[...] * (1.0 + sc_ref[...]) + sh_ref[...]).astype(BF16)

    cw = FFN_COLS

    def up(scr, j, col):
        scr[j, 0:HALO, :] = scr[j, tm:tm + HALO, :]
        scr[j, HALO:tm + HALO, :] = _dot(hb, wup_ref[:, col * cw:(col + 1) * cw])

    def conv(scr, j, col):
        cols = slice(col * cw, (col + 1) * cw)
        w = cw_ref[:, cols]
        out = cb_ref[:, cols]
        for k in range(CONV_WIDTH):
            off = HALO - (CONV_WIDTH - 1) + k
            out = out + w[k:k + 1, :] * scr[j, off:off + tm, :]
        return out

    y = jnp.zeros(x.shape, F32)
    for j in range(n_col):
        up(ua_scr, j, j)
        up(uv_scr, j, n_col + j)
    for j in range(n_col):
        a = conv(ua_scr, j, j)
        v = conv(uv_scr, j, n_col + j)
        act = (a * _sigmoid(a) * v).astype(BF16)
        y = y + _dot(act, wdn_ref[j * cw:(j + 1) * cw, :])
    x2 = x + gt_ref[...] * y
    o_ref[...] = _rms_scale(x2) * gf_ref[...]


def _conv_ffn(x1, norm_g, sc, sh, w_up, conv_w, conv_b, w_down, gt, normf_g):
    l, d = x1.shape
    f = w_down.shape[0]
    tm, cw = FFN_TILE, FFN_COLS
    n_col = f // cw
    assert f % cw == 0
    wup = w_up.astype(BF16)
    cbt = conv_b.reshape(1, 2 * f)
    wdn = w_down.astype(BF16)
    row = lambda i: (i, 0)
    const = lambda i: (0, 0)
    return pl.pallas_call(
        _ffn_kernel,
        grid=(l // tm,),
        in_specs=[pl.BlockSpec((tm, d), row),
                  pl.BlockSpec((1, d), const), pl.BlockSpec((1, d), const), pl.BlockSpec((1, d), const),
                  pl.BlockSpec(wup.shape, const), pl.BlockSpec(conv_w.shape, const),
                  pl.BlockSpec(cbt.shape, const), pl.BlockSpec(wdn.shape, const),
                  pl.BlockSpec((1, d), const), pl.BlockSpec((1, d), const)],
        out_specs=pl.BlockSpec((tm, d), row),
        out_shape=jax.ShapeDtypeStruct((l, d), F32),
        scratch_shapes=[pltpu.VMEM((n_col, tm + HALO, cw), F32),
                        pltpu.VMEM((n_col, tm + HALO, cw), F32)],
        compiler_params=_params(1),
        name="ffn",
    )(x1, norm_g.reshape(1, d), sc, sh, wup, conv_w, cbt, wdn, gt, normf_g.reshape(1, d))


def kernel(x, c, norm1_g, norm2_g, normf_g, w_ada, b_ada, w_in, pe_k, pe_v, w_ck1, w_ck2, w_cv1, w_cv2,
           lam_re, lam_im, log_step, b_re, b_im, c_re, c_im, d_skip, w_glu, b_glu, attn_norm_g,
           ssm_norm_g, w_o, w_up, conv_w, conv_b, w_down):
    b, l, d = x.shape
    depth = w_ada.shape[0]
    assert b == 1 and depth == 1
    assert l % KEY_CHUNK == 0 and l % ROW_TILE == 0 and l >= WINDOW
    x2 = x.reshape(l, d)
    mod = _modulation(c, w_ada[0], b_ada[0])
    sh1, sc1, gt1, sh2, sc2, gt2 = [mod[:, k * d:(k + 1) * d] for k in range(6)]

    qT, k_c, k_s, k_w, v_c, v_sT, v_wT, gT, zs = _in_projection(x2, norm1_g[0], sc1, sh1, w_in[0])
    kc = _compress(k_c, pe_k[0], w_ck1[0], w_ck2[0], transpose_out=False)
    vcT = _compress(v_c, pe_v[0], w_cv1[0], w_cv2[0], transpose_out=True)
    o_attn = _attention(qT, kc, vcT, k_s, v_sT, k_w, v_wT, gT, l)
    y_ssm = _s5(zs, lam_re[0], lam_im[0], log_step[0], b_re[0], b_im[0], c_re[0], c_im[0],
                d_skip[0], w_glu[0], b_glu[0])
    x1 = _out_projection(x2, o_attn, y_ssm, attn_norm_g[0], ssm_norm_g[0], w_o[0], gt1)
    out = _conv_ffn(x1, norm2_g[0], sc2, sh2, w_up[0], conv_w[0], conv_b[0], w_down[0], gt2, normf_g)
    return out.reshape(b, l, d)
```

```python
import functools
import math

import jax
import jax.numpy as jnp
import numpy as np
from jax.experimental import pallas as pl
from jax.experimental.pallas import tpu as pltpu

ATTN_HEADS = 8
KV_HEADS = 2
HEAD_DIM = 64
Q_PER_KV = ATTN_HEADS // KV_HEADS
ATTN_WIDTH = ATTN_HEADS * HEAD_DIM
KV_WIDTH = KV_HEADS * HEAD_DIM
SSM_GROUP = 16
SSM_STATE = 64
CMP_BLOCK = 32
CMP_STRIDE = 16
SEL_BLOCK = 64
SEL_TOPN = 16
WINDOW = 512
FORCE_SCORE = 1.0e4
ROPE_THETA = 10000.0
CONV_WIDTH = 3
NORM_EPS = 1e-6

LANES = 128
Q_TILE = 128
KEY_CHUNK = 512
ROW_TILE = 512
WIN_CHUNKS = WINDOW // Q_TILE + 1
VMEM_LIMIT = 56 * 1024 * 1024

F32 = jnp.float32
BF16 = jnp.bfloat16
NEG_INF = float("-inf")


def _params(n_axes, vmem=VMEM_LIMIT):
    return pltpu.CompilerParams(dimension_semantics=("arbitrary",) * n_axes,
                                vmem_limit_bytes=vmem)


def _split(a):
    hi = a.astype(BF16)
    lo = (a - hi.astype(F32)).astype(BF16)
    return hi, lo


def _dot(a, b):
    return jnp.dot(a, b, preferred_element_type=F32)


def _dot_hp(a, b):
    a_hi, a_lo = _split(a)
    b_hi, b_lo = _split(b)
    return _dot(a_hi, b_hi) + (_dot(a_hi, b_lo) + _dot(a_lo, b_hi))


def _sigmoid(x):
    return 1.0 / (1.0 + jnp.exp(-x))


def _gelu_tanh(x):
    c = math.sqrt(2.0 / math.pi)
    return 0.5 * x * (1.0 + jnp.tanh(c * (x + 0.044715 * (x * x * x))))


def _rms_scale(x):
    return x * jax.lax.rsqrt(jnp.mean(x * x, axis=-1, keepdims=True) + NORM_EPS)


def _mod_kernel(c_ref, w_ref, b_ref, o_ref):
    c = c_ref[...]
    s = c * _sigmoid(c)
    o_ref[...] = _dot_hp(s, w_ref[...]) + b_ref[...]


def _modulation(c, w_ada, b_ada):
    d, n = w_ada.shape
    bn = n // 4
    c8 = jnp.broadcast_to(c, (8, d))
    out = pl.pallas_call(
        _mod_kernel,
        grid=(n // bn,),
        in_specs=[pl.BlockSpec((8, d), lambda j: (0, 0)),
                  pl.BlockSpec((d, bn), lambda j: (0, j)),
                  pl.BlockSpec((1, bn), lambda j: (0, j))],
        out_specs=pl.BlockSpec((8, bn), lambda j: (0, j)),
        out_shape=jax.ShapeDtypeStruct((8, n), F32),
        compiler_params=_params(1),
        name="mod",
    )(c8, w_ada, b_ada.reshape(1, n))
    return out[0:1]


def _inproj_kernel(x_ref, g_ref, sc_ref, sh_ref, cos_ref, sin_ref, w_ref,
                   qT_ref, kc_ref, ks_ref, kw_ref, vc_ref, vsT_ref, vwT_ref, gT_ref, zs_ref):
    x = x_ref[...]
    h = _rms_scale(x) * g_ref[...] * (1.0 + sc_ref[...]) + sh_ref[...]
    hb = h.astype(BF16)
    cos = cos_ref[...]
    sin = sin_ref[...]
    n_sub = x.shape[0] // Q_TILE
    widths = (ATTN_WIDTH, 3 * KV_WIDTH, 3 * KV_WIDTH, LANES, w_ref.shape[1])
    starts = np.concatenate([[0], np.cumsum(widths[:-1])])
    wq_ref, wk_ref, wv_ref, wg_ref, ws_ref = [
        w_ref.at[:, int(a):int(min(a + n, w_ref.shape[1]))] for a, n in zip(starts, widths)]

    half = HEAD_DIM // 2
    first_half = (jax.lax.broadcasted_iota(jnp.int32, (1, LANES), 1) & (HEAD_DIM - 1)) < half

    def rope(z):
        swapped = jnp.where(first_half, pltpu.roll(z, LANES - half, 1), pltpu.roll(z, half, 1))
        return z * cos + swapped * sin

    zq = _dot(hb, wq_ref[...])
    scale = HEAD_DIM ** -0.5 * math.log2(math.e)
    q_cols = []
    for j in range(ATTN_WIDTH // LANES):
        sl = slice(j * LANES, (j + 1) * LANES)
        q_cols.append(rope(zq[:, sl]) * scale)
    q = jnp.concatenate(q_cols, axis=1)
    for s in range(n_sub):
        qs = q[s * Q_TILE:(s + 1) * Q_TILE, :].T
        qT_ref[s] = qs.reshape(KV_HEADS, Q_PER_KV * HEAD_DIM, Q_TILE)

    zk = _dot(hb, wk_ref[...])
    k_outs = (kc_ref, ks_ref, kw_ref)
    for j in range(3):
        sl = slice(j * LANES, (j + 1) * LANES)
        k_outs[j][...] = rope(zk[:, sl]).astype(k_outs[j].dtype)

    zv = _dot(hb, wv_ref[...])
    vc_ref[...] = zv[:, 0:LANES]
    vs = zv[:, LANES:2 * LANES]
    vw = zv[:, 2 * LANES:3 * LANES]
    for s in range(n_sub):
        vsT_ref[s] = vs[s * Q_TILE:(s + 1) * Q_TILE, :].T.astype(BF16)
        vwT_ref[s] = vw[s * Q_TILE:(s + 1) * Q_TILE, :].T.astype(BF16)

    gates = _sigmoid(_dot(hb, wg_ref[...]))
    for s in range(n_sub):
        gt = gates[s * Q_TILE:(s + 1) * Q_TILE, :].T
        gT_ref[s] = gt[0:2 * 16, :].reshape(KV_HEADS, 16, Q_TILE)

    zs_ref[...] = _dot(hb, ws_ref[...])


def _in_projection(x2, norm_g, sc, sh, w_in):
    l, d = x2.shape
    tm = ROW_TILE
    cuts = np.cumsum([0, ATTN_WIDTH] + [KV_WIDTH] * 6 + [3 * ATTN_HEADS])
    q_cols = np.arange(cuts[0], cuts[1])
    k_cols = np.concatenate([np.arange(cuts[j], cuts[j + 1]) for j in (1, 3, 5)])
    v_cols = np.concatenate([np.arange(cuts[j], cuts[j + 1]) for j in (2, 4, 6)])
    s_cols = np.arange(cuts[8], w_in.shape[1])
    ssm_w = len(s_cols)

    g_cols = np.zeros((LANES,), np.int64)
    g_sign = np.zeros((LANES,), np.int64)
    per_kv = Q_PER_KV * 3
    for hh in range(KV_HEADS):
        g_cols[hh * 16:hh * 16 + per_kv] = cuts[7] + hh * per_kv + np.arange(per_kv)
        g_sign[hh * 16:hh * 16 + per_kv] = 1
    cols = np.concatenate([q_cols, k_cols, v_cols, g_cols, s_cols])
    sign = np.concatenate([np.ones_like(q_cols), np.ones_like(k_cols), np.ones_like(v_cols), g_sign,
                           np.ones_like(s_cols)])
    w_all = (jnp.take(w_in, jnp.asarray(cols, jnp.int32), axis=1)
             * jnp.asarray(sign, F32)[None, :]).astype(BF16)
    weights = [w_all]

    half = HEAD_DIM // 2
    inv = jnp.power(ROPE_THETA, -jnp.arange(half, dtype=F32) * 2.0 / HEAD_DIM)
    ang = jnp.arange(l, dtype=F32)[:, None] * inv[None, :]
    cos = jnp.tile(jnp.cos(ang), (1, LANES // half))
    sin = jnp.tile(jnp.concatenate([-jnp.sin(ang), jnp.sin(ang)], axis=1), (1, LANES // HEAD_DIM))

    n_q = l // Q_TILE
    sub = tm // Q_TILE
    row = lambda i: (i, 0)
    const = lambda i: (0, 0)
    out_shape = [
        jax.ShapeDtypeStruct((n_q, KV_HEADS, Q_PER_KV * HEAD_DIM, Q_TILE), F32),
        jax.ShapeDtypeStruct((l, KV_WIDTH), F32),
        jax.ShapeDtypeStruct((l, KV_WIDTH), BF16),
        jax.ShapeDtypeStruct((l, KV_WIDTH), BF16),
        jax.ShapeDtypeStruct((l, KV_WIDTH), F32),
        jax.ShapeDtypeStruct((n_q, KV_WIDTH, Q_TILE), BF16),
        jax.ShapeDtypeStruct((n_q, KV_WIDTH, Q_TILE), BF16),
        jax.ShapeDtypeStruct((n_q, KV_HEADS, 16, Q_TILE), F32),
        jax.ShapeDtypeStruct((l, ssm_w), F32),
    ]
    out_specs = [
        pl.BlockSpec((sub, KV_HEADS, Q_PER_KV * HEAD_DIM, Q_TILE), lambda i: (i, 0, 0, 0)),
        pl.BlockSpec((tm, KV_WIDTH), row),
        pl.BlockSpec((tm, KV_WIDTH), row),
        pl.BlockSpec((tm, KV_WIDTH), row),
        pl.BlockSpec((tm, KV_WIDTH), row),
        pl.BlockSpec((sub, KV_WIDTH, Q_TILE), lambda i: (i, 0, 0)),
        pl.BlockSpec((sub, KV_WIDTH, Q_TILE), lambda i: (i, 0, 0)),
        pl.BlockSpec((sub, KV_HEADS, 16, Q_TILE), lambda i: (i, 0, 0, 0)),
        pl.BlockSpec((tm, ssm_w), row),
    ]
    in_specs = [pl.BlockSpec((tm, d), row),
                pl.BlockSpec((1, d), const), pl.BlockSpec((1, d), const), pl.BlockSpec((1, d), const),
                pl.BlockSpec((tm, LANES), row), pl.BlockSpec((tm, LANES), row)]
    in_specs += [pl.BlockSpec(w.shape, const) for w in weights]
    return pl.pallas_call(
        _inproj_kernel,
        grid=(l // tm,),
        in_specs=in_specs,
        out_specs=out_specs,
        out_shape=out_shape,
        compiler_params=_params(1),
        name="inproj",
    )(x2, norm_g.reshape(1, d), sc, sh, cos, sin, *weights)


def _compress_kernel(z_ref, pe_ref, w1_ref, w1p_ref, w2_ref, o_ref, *, transpose_out):
    n = z_ref.shape[0] // CMP_STRIDE
    first = jnp.zeros((n, w1_ref.shape[1]), F32)
    second = jnp.zeros((n, w1_ref.shape[1]), F32)
    for j in range(CMP_STRIDE):
        x_j = z_ref[pl.ds(j, n, stride=CMP_STRIDE), :].astype(BF16)
        first = first + _dot(x_j, w1p_ref[0, j])
        second = second + _dot(x_j, w1p_ref[0, CMP_STRIDE + j])
    bias = _dot(pe_ref[...].astype(BF16), w1_ref[...])[0:1]
    h1 = first + pltpu.roll(second, n - 1, 0) + bias
    out = _dot_hp(_gelu_tanh(h1), w2_ref[...])
    if transpose_out:
        o_ref[0] = out.T[0:HEAD_DIM, :]
    else:
        o_ref[0] = out[:, 0:HEAD_DIM]


def _compress(z, pe, w1, w2, transpose_out):
    l = z.shape[0]
    n = l // CMP_STRIDE
    hidden = w1.shape[1]
    assert CMP_BLOCK == 2 * CMP_STRIDE
    pe8 = jnp.broadcast_to(pe.reshape(1, CMP_BLOCK * HEAD_DIM), (8, CMP_BLOCK * HEAD_DIM))
    w1b = w1.astype(BF16)
    w1r = w1b.reshape(CMP_BLOCK, 1, HEAD_DIM, hidden)
    head_sel = jnp.eye(KV_HEADS, dtype=BF16)[:, None, :, None, None]
    w1p = (head_sel * w1r[None]).reshape(KV_HEADS, CMP_BLOCK, KV_WIDTH, hidden)
    w2p = jnp.pad(w2, ((0, 0), (0, LANES - HEAD_DIM)))
    oshape = (KV_HEADS, HEAD_DIM, n) if transpose_out else (KV_HEADS, n, HEAD_DIM)
    return pl.pallas_call(
        functools.partial(_compress_kernel, transpose_out=transpose_out),
        grid=(KV_HEADS,),
        in_specs=[pl.BlockSpec(z.shape, lambda h: (0, 0)),
                  pl.BlockSpec(pe8.shape, lambda h: (0, 0)),
                  pl.BlockSpec(w1b.shape, lambda h: (0, 0)),
                  pl.BlockSpec((1,) + w1p.shape[1:], lambda h: (h, 0, 0, 0)),
                  pl.BlockSpec(w2p.shape, lambda h: (0, 0))],
        out_specs=pl.BlockSpec((1,) + oshape[1:], lambda h: (h, 0, 0)),
        out_shape=jax.ShapeDtypeStruct(oshape, F32),
        compiler_params=_params(1),
        name="compress_v" if transpose_out else "compress_k",
    )(z, pe8, w1b, w1p, w2p)


def _tile4(a):
    return jnp.concatenate([a] * Q_PER_KV, axis=1)


MASKED = -1e30
P_PAD = 8


LOOP_STEPS = 2
SEL_STEP = 64
N_FORCED = 3


def _attn_kernel(qT_ref, kc_ref, vcT_ref, ks_ref, vsT_ref, *rest):
    kw_refs = rest[0:WIN_CHUNKS]
    vw_refs = rest[WIN_CHUNKS:2 * WIN_CHUNKS]
    gT_ref, o_ref, p_scr, bias_scr, own_scr, oc_scr, ow_scr, stage_scr = rest[2 * WIN_CHUNKS:2 * WIN_CHUNKS + 8]
    per_head = rest[2 * WIN_CHUNKS + 8:]
    H, G, DH, TQ = KV_HEADS, Q_PER_KV, HEAD_DIM, Q_TILE
    s_bufs = [per_head[k * H:(k + 1) * H] for k in range(LOOP_STEPS)]
    p_bufs = [per_head[(LOOP_STEPS + k) * H:(LOOP_STEPS + k + 1) * H] for k in range(LOOP_STEPS)]
    i = pl.program_id(0)
    n_sel = bias_scr.shape[0]
    W = G * TQ
    heads = range(H)

    q_all, q_pad = [], []
    for h in heads:
        qT = qT_ref[0, h]
        qa = jnp.concatenate([qT[g * DH:(g + 1) * DH, :] for g in range(G)], axis=1)
        zeros = jnp.zeros_like(qa)
        q_all.append(qa)
        q_pad.append(jnp.concatenate([qa if hh == h else zeros for hh in heads], axis=0).astype(BF16))
    v_rows = [slice(h * DH, (h + 1) * DH) for h in heads]

    t = i * TQ + jax.lax.broadcasted_iota(jnp.int32, (1, TQ), 1)
    cur = jax.lax.shift_right_logical(t, int(math.log2(SEL_BLOCK)))

    ratio = SEL_BLOCK // CMP_STRIDE

    def select(ns):
        nc = ns * ratio
        c_end = jax.lax.broadcasted_iota(jnp.int32, (nc, 1), 0) * CMP_STRIDE + (CMP_BLOCK - 1)
        c_bias = _tile4(jnp.where(c_end <= t, 0.0, MASKED))
        blk = jax.lax.broadcasted_iota(jnp.int32, (ns, 1), 0)
        blk_f = blk.astype(F32)
        forced = (blk == 0) | (blk == cur) | (blk == cur - 1)
        scores = []
        for h in heads:
            stage_scr[h, 0:nc, :] = _dot(kc_ref[h, 0:nc, :].astype(BF16), q_all[h].astype(BF16))
        for h in heads:
            s = stage_scr[h, 0:nc, :] + c_bias
            m = jnp.max(s, axis=0, keepdims=True)
            e = jnp.exp2(s - m)
            any_valid = jnp.where(m > 0.5 * MASKED, 1.0, 0.0)
            p = e * (any_valid / jnp.sum(e, axis=0, keepdims=True))
            oc_scr[h] = _dot(vcT_ref[h, :, 0:nc].astype(BF16), p.astype(BF16))
            p_sum = p[:, 0:TQ]
            for g in range(1, G):
                p_sum = p_sum + p[:, g * TQ:(g + 1) * TQ]
            p_scr[h, 0:P_PAD, :] = jnp.zeros((P_PAD, TQ), F32)
            p_scr[h, P_PAD:P_PAD + nc, :] = p_sum

            def every(off):
                return p_scr[h, pl.ds(P_PAD + off, ns, stride=ratio), :]

            imp = 0.5 * (every(-1) + every(3)) + (every(0) + every(1) + every(2))
            scores.append(jnp.where((blk > cur) | forced, NEG_INF, imp))
        score = jnp.concatenate(scores, axis=1)

        def pick_one(_, score):
            m = jnp.max(score, axis=0, keepdims=True)
            first = jnp.min(jnp.where(score == m, blk_f, float(ns)), axis=0, keepdims=True)
            return jnp.where(blk_f == first, NEG_INF, score)

        left = jax.lax.fori_loop(0, SEL_TOPN - N_FORCED, pick_one, score)
        cur_w = jnp.concatenate([cur] * H, axis=1)
        bias = jnp.where((left == NEG_INF) & (blk <= cur_w), 0.0, MASKED)
        own_scr[0:1, :] = jnp.max(jnp.where(blk == 2 * i, bias, MASKED), axis=0, keepdims=True)
        own_scr[1:2, :] = jnp.max(jnp.where(blk == 2 * i + 1, bias, MASKED), axis=0, keepdims=True)
        bias_scr[0:ns, :] = jnp.where((blk == 2 * i) | (blk == 2 * i + 1), MASKED, bias)
        if ns < n_sel:
            bias_scr[ns:n_sel, :] = jnp.full((n_sel - ns, H * TQ), MASKED, F32)

    sizes = [min(n_sel, SEL_STEP * (k + 1)) for k in range(pl.cdiv(n_sel, SEL_STEP))]
    variant = jnp.minimum((2 * i + 1) // SEL_STEP, len(sizes) - 1)
    for k, ns in enumerate(sizes):
        pl.when(variant == k)(functools.partial(select, ns))

    KC = KEY_CHUNK
    BPC = KC // SEL_BLOCK
    VROWS = DH + 16
    onehot = (jax.lax.broadcasted_iota(jnp.int32, (KC, LANES), 1)
              == jax.lax.shift_right_logical(jax.lax.broadcasted_iota(jnp.int32, (KC, LANES), 0),
                                             int(math.log2(SEL_BLOCK)))).astype(BF16)
    ones_rows = jnp.ones((VROWS - DH, KC), BF16)
    pad_rows = jnp.zeros((LANES - BPC, W), F32)
    last_chunk = ks_ref.shape[0] // KC - 1

    n_main = (i + KC // TQ - 1) // (KC // TQ)

    def augmented_q(c):
        rows = bias_scr[pl.ds(pl.multiple_of(jnp.minimum(c, last_chunk) * BPC, BPC), BPC), :]
        rows = jnp.where(c < n_main, rows, MASKED)
        out = []
        for h in heads:
            extra = jnp.concatenate([_tile4(rows[:, h * TQ:(h + 1) * TQ]), pad_rows], axis=0)
            out.append(jnp.concatenate([q_pad[h], extra.astype(BF16)], axis=0))
        return out

    def scores(c):
        k = ks_ref[pl.ds(pl.multiple_of(jnp.minimum(c, last_chunk) * KC, KC), KC), :]
        k_aug = jnp.concatenate([k, onehot], axis=1)
        q_aug = augmented_q(c)
        out = []
        for h in heads:
            s = _dot(k_aug, q_aug[h])
            out.append((s, jnp.max(s, axis=0, keepdims=True)))
        return out

    def values(c):
        c = jnp.clip(c, 0, last_chunk)
        v = jnp.concatenate([vsT_ref[(KC // TQ) * c + j] for j in range(KC // TQ)], axis=1)
        return [jnp.concatenate([v[v_rows[h], :], ones_rows], axis=0) for h in heads]

    def chunk_step(c, m_run, acc, mx_cur, s_cur, s_next, p_prev, p_cur):
        v_prev = values(c - 1)
        nxt = scores(c + 1)
        m_new, mx_next = [], []
        for h in heads:
            s_next[h][...] = nxt[h][0]
            mx_next.append(nxt[h][1])
        for h in heads:
            acc[h] = acc[h] + _dot(v_prev[h], p_prev[h][...])
        for h in heads:
            m_new.append(jnp.maximum(m_run[h], mx_cur[h]))
            acc[h] = jnp.exp2(m_run[h] - m_new[h]) * acc[h]
            p_cur[h][...] = jnp.exp2(s_cur[h][...] - m_new[h]).astype(BF16)
        return m_new, acc, mx_next

    B = LOOP_STEPS

    def body(j, carry):
        m_run, acc, mx = (list(c) for c in carry)
        for k in range(B):
            m_run, acc, mx = chunk_step(B * j + k, m_run, acc, mx,
                                        s_bufs[k], s_bufs[(k + 1) % B], p_bufs[(k - 1) % B], p_bufs[k])
        return tuple(m_run), tuple(acc), tuple(mx)

    n_iter = (n_main + B - 1) // B

    k_own = ks_ref[pl.ds(pl.multiple_of(i * TQ, TQ), TQ), :]
    v_own = vsT_ref[i]
    k_w = jnp.concatenate([r[...] for r in kw_refs], axis=0)
    v_w = jnp.concatenate([r[0] for r in vw_refs], axis=1)
    n_win = WIN_CHUNKS * TQ
    w_pos = (i - (WIN_CHUNKS - 1)) * TQ + jax.lax.broadcasted_iota(jnp.int32, (n_win, 1), 0)
    w_bias = _tile4(jnp.where((w_pos <= t) & (w_pos > t - WINDOW) & (w_pos >= 0), 0.0, MASKED))
    ones_w = jnp.ones((VROWS - DH, n_win), BF16)
    for h in heads:
        stage_scr[h, 0:TQ, :] = _dot(k_own, q_pad[h])
        stage_scr[h, TQ:TQ + n_win, :] = _dot(k_w, q_pad[h])
    s_0 = scores(0)
    for h in heads:
        s_bufs[0][h][...] = s_0[h][0]
        p_bufs[B - 1][h][...] = jnp.zeros(p_bufs[B - 1][h].shape, BF16)
    for h in heads:
        s = stage_scr[h, TQ:TQ + n_win, :] + w_bias
        p = jnp.exp2(s - jnp.max(s, axis=0, keepdims=True)).astype(BF16)
        a_w = _dot(jnp.concatenate([v_w[v_rows[h], :], ones_w], axis=0), p)
        ow_scr[h] = a_w[0:DH, :] / a_w[DH:DH + 1, :]

    init = (tuple(jnp.full((1, W), MASKED, F32) for _ in heads),
            tuple(jnp.zeros((VROWS, W), F32) for _ in heads),
            tuple(s_0[h][1] for h in heads))
    m_run, acc, _ = jax.lax.fori_loop(0, n_iter, body, init)
    v_last = values(B * n_iter - 1)
    acc = tuple(acc[h] + _dot(v_last[h], p_bufs[B - 1][h][...]) for h in heads)

    row_o = jax.lax.broadcasted_iota(jnp.int32, (TQ, 1), 0)
    lane_o = jax.lax.broadcasted_iota(jnp.int32, (1, TQ), 1)
    head_out = []
    for h in heads:
        a = acc[h]
        own_lo = own_scr[0:1, h * TQ:(h + 1) * TQ]
        own_hi = own_scr[1:2, h * TQ:(h + 1) * TQ]
        own_bias = jnp.where(row_o <= lane_o, jnp.where(row_o < SEL_BLOCK, own_lo, own_hi), MASKED)
        s = stage_scr[h, 0:TQ, :] + _tile4(own_bias)
        m_s = jnp.maximum(m_run[h], jnp.max(s, axis=0, keepdims=True))
        p = jnp.exp2(s - m_s).astype(BF16)
        a = jnp.exp2(m_run[h] - m_s) * a + _dot(jnp.concatenate([v_own[v_rows[h], :], ones_rows[:, 0:TQ]], axis=0), p)
        o_s = a[0:DH, :] / a[DH:DH + 1, :]

        o_w = ow_scr[h]
        o_c = oc_scr[h]
        gates = gT_ref[0, h]
        for g in range(G):
            cols = slice(g * TQ, (g + 1) * TQ)
            head_out.append(gates[3 * g:3 * g + 1, :] * o_c[:, cols]
                            + gates[3 * g + 1:3 * g + 2, :] * o_s[:, cols]
                            + gates[3 * g + 2:3 * g + 3, :] * o_w[:, cols])
    o_ref[...] = jnp.concatenate(head_out, axis=0).T


def _attention(qT, kc, vcT, ks, vsT, kw, vwT, gT, l):
    n_q = l // Q_TILE
    n_cmp = l // CMP_STRIDE
    n_sel = l // SEL_BLOCK
    assert CMP_BLOCK == 2 * CMP_STRIDE and SEL_BLOCK == 4 * CMP_STRIDE

    assert n_sel >= SEL_TOPN and KV_WIDTH == LANES
    H, W = KV_HEADS, Q_PER_KV * Q_TILE

    def win(j):
        return lambda i: (jnp.maximum(i - (WIN_CHUNKS - 1) + j, 0), 0)

    def win3(j):
        return lambda i: (jnp.maximum(i - (WIN_CHUNKS - 1) + j, 0), 0, 0)

    in_specs = [
        pl.BlockSpec((1, H, Q_PER_KV * HEAD_DIM, Q_TILE), lambda i: (i, 0, 0, 0)),
        pl.BlockSpec(kc.shape, lambda i: (0, 0, 0)),
        pl.BlockSpec(vcT.shape, lambda i: (0, 0, 0)),
        pl.BlockSpec(ks.shape, lambda i: (0, 0)),
        pl.BlockSpec(vsT.shape, lambda i: (0, 0, 0)),
    ]
    in_specs += [pl.BlockSpec((Q_TILE, KV_WIDTH), win(j)) for j in range(WIN_CHUNKS)]
    in_specs += [pl.BlockSpec((1, KV_WIDTH, Q_TILE), win3(j)) for j in range(WIN_CHUNKS)]
    in_specs += [pl.BlockSpec((1, H, 16, Q_TILE), lambda i: (i, 0, 0, 0))]
    return pl.pallas_call(
        _attn_kernel,
        grid=(n_q,),
        in_specs=in_specs,
        out_specs=pl.BlockSpec((Q_TILE, ATTN_WIDTH), lambda i: (i, 0)),
        out_shape=jax.ShapeDtypeStruct((l, ATTN_WIDTH), F32),
        scratch_shapes=[pltpu.VMEM((H, P_PAD + n_cmp, Q_TILE), F32),
                        pltpu.VMEM((n_sel, H * Q_TILE), F32),
                        pltpu.VMEM((8, H * Q_TILE), F32),
                        pltpu.VMEM((H, HEAD_DIM, W), F32),
                        pltpu.VMEM((H, HEAD_DIM, W), F32),
                        pltpu.VMEM((H, max(n_cmp, (WIN_CHUNKS + 1) * Q_TILE), W), F32)]
        + [pltpu.VMEM((KEY_CHUNK, W), F32)] * (LOOP_STEPS * H)
        + [pltpu.VMEM((KEY_CHUNK, W), BF16)] * (LOOP_STEPS * H),
        compiler_params=_params(1),
        name="attn",
    )(qT, kc, vcT, ks, vsT, *([kw] * WIN_CHUNKS), *([vwT] * WIN_CHUNKS), gT)


SSM_TILE = 256
SSM_PACK = 16


def _ssm_kernel(u_ref, a_ref, wb_ref, wc_ref, msk_ref, d_ref, wg_ref, bg_ref, y_ref, x_scr, st_scr):
    step = pl.program_id(0)
    tc, width = u_ref.shape
    P = SSM_PACK

    @pl.when(step == 0)
    def _():
        st_scr[...] = jnp.zeros_like(st_scr)

    PH = P // 2
    hw = width // 2
    msk = msk_ref[...]
    u = u_ref[...]

    for hf in range(2):
        u_half = u[:, hf * hw:(hf + 1) * hw]
        wb_half = wb_ref[hf * hw:(hf + 1) * hw, :]
        for j in range(PH):
            bu = _dot((u_half * msk[j:j + 1, :]).astype(BF16), wb_half)
            x_scr[hf, 0, pl.ds(j, tc, stride=PH), :] = bu[:, 0:LANES]
            x_scr[hf, 1, pl.ds(j, tc, stride=PH), :] = bu[:, LANES:2 * LANES]

    a_re = [a_ref[0, hf * PH:(hf + 1) * PH, :] for hf in range(2)]
    a_im = [a_ref[1, hf * PH:(hf + 1) * PH, :] for hf in range(2)]

    def one(ti, carry):
        rows = pl.ds(pl.multiple_of(ti * PH, PH), PH)
        out = []
        for hf in range(2):
            x_re, x_im = carry[hf]
            n_re = a_re[hf] * x_re - a_im[hf] * x_im + x_scr[hf, 0, rows, :]
            n_im = a_re[hf] * x_im + a_im[hf] * x_re + x_scr[hf, 1, rows, :]
            x_scr[hf, 0, rows, :] = n_re
            x_scr[hf, 1, rows, :] = n_im
            out.append((n_re, n_im))
        return tuple(out)

    init = tuple((st_scr[0, hf * PH:(hf + 1) * PH, :], st_scr[1, hf * PH:(hf + 1) * PH, :]) for hf in range(2))
    last = jax.lax.fori_loop(0, tc, one, init, unroll=8)
    for hf in range(2):
        st_scr[0, hf * PH:(hf + 1) * PH, :] = last[hf][0]
        st_scr[1, hf * PH:(hf + 1) * PH, :] = last[hf][1]

    halves = []
    for hf in range(2):
        wc_half = wc_ref[:, hf * hw:(hf + 1) * hw]
        y_half = jnp.zeros((tc, hw), F32)
        for j in range(PH):
            x_j = jnp.concatenate([x_scr[hf, 0, pl.ds(j, tc, stride=PH), :],
                                   x_scr[hf, 1, pl.ds(j, tc, stride=PH), :]], axis=1).astype(BF16)
            y_half = y_half + msk[j:j + 1, :] * _dot(x_j, wc_half)
        halves.append(y_half)
    y = _gelu_tanh(jnp.concatenate(halves, axis=1) + d_ref[...] * u)
    y_ref[...] = y * _sigmoid(_dot(y.astype(BF16), wg_ref[...]) + bg_ref[...])


def _s5(zs, lam_re, lam_im, log_step, b_re, b_im, c_re, c_im, d_skip, w_glu, b_glu):
    l, width = zs.shape
    G, Pn, C = b_re.shape
    assert Pn == SSM_STATE and C == SSM_GROUP and G * C == width and G == 2 * SSM_PACK
    step = jnp.exp(log_step)[:, None]
    mag = jnp.exp(lam_re * step)
    a_re = mag * jnp.cos(lam_im * step)
    a_im = mag * jnp.sin(lam_im * step)
    den = lam_re * lam_re + lam_im * lam_im
    n_re = a_re - 1.0
    f_re = (n_re * lam_re + a_im * lam_im) / den
    f_im = (a_im * lam_re - n_re * lam_im) / den
    bb_re = f_re[..., None] * b_re - f_im[..., None] * b_im
    bb_im = f_re[..., None] * b_im + f_im[..., None] * b_re
    a_pack = jnp.stack([a_re.reshape(SSM_PACK, 2 * Pn), a_im.reshape(SSM_PACK, 2 * Pn)])

    eye2 = jnp.eye(2, dtype=F32)

    def b_stack(bb):
        t = bb.reshape(SSM_PACK, 2, Pn, C)
        return jnp.einsum('jgpc,gh->jhcgp', t, eye2).reshape(width, 2 * Pn)

    w_b = jnp.concatenate([b_stack(bb_re), b_stack(bb_im)], axis=1).astype(BF16)

    def c_stack(cc):
        t = cc.reshape(SSM_PACK, 2, C, Pn)
        return jnp.einsum('jgcp,gh->gpjhc', t, eye2).reshape(2 * Pn, width)

    w_c = jnp.concatenate([c_stack(c_re), -c_stack(c_im)], axis=0).astype(BF16)
    half_rows, half_w = SSM_PACK // 2, width // 2
    lane_row = jnp.arange(half_w) // (2 * C)
    msk = (lane_row[None, :] == jnp.arange(half_rows)[:, None]).astype(F32)

    tc = SSM_TILE
    const2 = lambda i: (0, 0)
    return pl.pallas_call(
        _ssm_kernel,
        grid=(l // tc,),
        in_specs=[pl.BlockSpec((tc, width), lambda i: (i, 0)),
                  pl.BlockSpec(a_pack.shape, lambda i: (0, 0, 0)),
                  pl.BlockSpec(w_b.shape, const2),
                  pl.BlockSpec(w_c.shape, const2),
                  pl.BlockSpec(msk.shape, const2),
                  pl.BlockSpec((1, width), const2),
                  pl.BlockSpec((width, width), const2),
                  pl.BlockSpec((1, width), const2)],
        out_specs=pl.BlockSpec((tc, width), lambda i: (i, 0)),
        out_shape=jax.ShapeDtypeStruct((l, width), F32),
        scratch_shapes=[pltpu.VMEM((2, 2, tc * half_rows, LANES), F32),
                        pltpu.VMEM((2, SSM_PACK, LANES), F32)],
        compiler_params=_params(1),
        name="ssm",
    )(zs, a_pack, w_b, w_c, msk, d_skip.reshape(1, width), w_glu.astype(BF16), b_glu.reshape(1, width))


def _outproj_kernel(x_ref, oa_ref, ys_ref, ga_ref, gs_ref, wa_ref, ws_ref, gt_ref, o_ref):
    a = (_rms_scale(oa_ref[...]) * ga_ref[...]).astype(BF16)
    s = (_rms_scale(ys_ref[...]) * gs_ref[...]).astype(BF16)
    y = _dot(a, wa_ref[...]) + _dot(s, ws_ref[...])
    o_ref[...] = x_ref[...] + gt_ref[...] * y


def _out_projection(x2, o_attn, y_ssm, attn_g, ssm_g, w_o, gt):
    l, d = x2.shape
    tm = ROW_TILE
    aw, sw = o_attn.shape[1], y_ssm.shape[1]
    row = lambda i: (i, 0)
    const = lambda i: (0, 0)
    return pl.pallas_call(
        _outproj_kernel,
        grid=(l // tm,),
        in_specs=[pl.BlockSpec((tm, d), row), pl.BlockSpec((tm, aw), row), pl.BlockSpec((tm, sw), row),
                  pl.BlockSpec((1, aw), const), pl.BlockSpec((1, sw), const),
                  pl.BlockSpec((aw, d), const), pl.BlockSpec((sw, d), const), pl.BlockSpec((1, d), const)],
        out_specs=pl.BlockSpec((tm, d), row),
        out_shape=jax.ShapeDtypeStruct((l, d), F32),
        compiler_params=_params(1),
        name="outproj",
    )(x2, o_attn, y_ssm, attn_g.reshape(1, aw), ssm_g.reshape(1, sw),
      w_o[:aw].astype(BF16), w_o[aw:].astype(BF16), gt)


FFN_TILE = 256
FFN_COLS = 256
HALO = 8


def _ffn_kernel(x_ref, g_ref, sc_ref, sh_ref, wup_ref, cw_ref, cb_ref, wdn_ref, gt_ref, gf_ref,
                o_ref, ua_scr, uv_scr):
    step = pl.program_id(0)
    tm = x_ref.shape[0]
    n_col = wdn_ref.shape[0] // FFN_COLS

    @pl.when(step == 0)
    def _():
        ua_scr[...] = jnp.zeros_like(ua_scr)
        uv_scr[...] = jnp.zeros_like(uv_scr)

    x = x_ref[...]
    hb = (_rms_scale(x) * g_ref[...] * (1.0 + sc_ref[...]) + sh_ref[...]).astype(BF16)

    cw = FFN_COLS

    def up(scr, j, col):
        scr[j, 0:HALO, :] = scr[j, tm:tm + HALO, :]
        scr[j, HALO:tm + HALO, :] = _dot(hb, wup_ref[:, col * cw:(col + 1) * cw])

    def conv(scr, j, col):
        cols = slice(col * cw, (col + 1) * cw)
        w = cw_ref[:, cols]
        out = cb_ref[:, cols]
        for k in range(CONV_WIDTH):
            off = HALO - (CONV_WIDTH - 1) + k
            out = out + w[k:k + 1, :] * scr[j, off:off + tm, :]
        return out

    y = jnp.zeros(x.shape, F32)
    for j in range(n_col):
        up(ua_scr, j, j)
        up(uv_scr, j, n_col + j)
    for j in range(n_col):
        a = conv(ua_scr, j, j)
        v = conv(uv_scr, j, n_col + j)
        act = (a * _sigmoid(a) * v).astype(BF16)
        y = y + _dot(act, wdn_ref[j * cw:(j + 1) * cw, :])
    x2 = x + gt_ref[...] * y
    o_ref[...] = _rms_scale(x2) * gf_ref[...]


def _conv_ffn(x1, norm_g, sc, sh, w_up, conv_w, conv_b, w_down, gt, normf_g):
    l, d = x1.shape
    f = w_down.shape[0]
    tm, cw = FFN_TILE, FFN_COLS
    n_col = f // cw
    assert f % cw == 0
    wup = w_up.astype(BF16)
    cbt = conv_b.reshape(1, 2 * f)
    wdn = w_down.astype(BF16)
    row = lambda i: (i, 0)
    const = lambda i: (0, 0)
    return pl.pallas_call(
        _ffn_kernel,
        grid=(l // tm,),
        in_specs=[pl.BlockSpec((tm, d), row),
                  pl.BlockSpec((1, d), const), pl.BlockSpec((1, d), const), pl.BlockSpec((1, d), const),
                  pl.BlockSpec(wup.shape, const), pl.BlockSpec(conv_w.shape, const),
                  pl.BlockSpec(cbt.shape, const), pl.BlockSpec(wdn.shape, const),
                  pl.BlockSpec((1, d), const), pl.BlockSpec((1, d), const)],
        out_specs=pl.BlockSpec((tm, d), row),
        out_shape=jax.ShapeDtypeStruct((l, d), F32),
        scratch_shapes=[pltpu.VMEM((n_col, tm + HALO, cw), F32),
                        pltpu.VMEM((n_col, tm + HALO, cw), F32)],
        compiler_params=_params(1),
        name="ffn",
    )(x1, norm_g.reshape(1, d), sc, sh, wup, conv_w, cbt, wdn, gt, normf_g.reshape(1, d))


def kernel(x, c, norm1_g, norm2_g, normf_g, w_ada, b_ada, w_in, pe_k, pe_v, w_ck1, w_ck2, w_cv1, w_cv2,
           lam_re, lam_im, log_step, b_re, b_im, c_re, c_im, d_skip, w_glu, b_glu, attn_norm_g,
           ssm_norm_g, w_o, w_up, conv_w, conv_b, w_down):
    b, l, d = x.shape
    depth = w_ada.shape[0]
    assert b == 1 and depth == 1
    assert l % KEY_CHUNK == 0 and l % ROW_TILE == 0 and l >= WINDOW
    x2 = x.reshape(l, d)
    mod = _modulation(c, w_ada[0], b_ada[0])
    sh1, sc1, gt1, sh2, sc2, gt2 = [mod[:, k * d:(k + 1) * d] for k in range(6)]

    qT, k_c, k_s, k_w, v_c, v_sT, v_wT, gT, zs = _in_projection(x2, norm1_g[0], sc1, sh1, w_in[0])
    kc = _compress(k_c, pe_k[0], w_ck1[0], w_ck2[0], transpose_out=False)
    vcT = _compress(v_c, pe_v[0], w_cv1[0], w_cv2[0], transpose_out=True)
    o_attn = _attention(qT, kc, vcT, k_s, v_sT, k_w, v_wT, gT, l)
    y_ssm = _s5(zs, lam_re[0], lam_im[0], log_step[0], b_re[0], b_im[0], c_re[0], c_im[0],
                d_skip[0], w_glu[0], b_glu[0])
    x1 = _out_projection(x2, o_attn, y_ssm, attn_norm_g[0], ssm_norm_g[0], w_o[0], gt1)
    out = _conv_ffn(x1, norm2_g[0], sc2, sh2, w_up[0], conv_w[0], conv_b[0], w_down[0], gt2, normf_g)
    return out.reshape(b, l, d)
```

```python
import functools
import math

import jax
import jax.numpy as jnp
import numpy as np
from jax.experimental import pallas as pl
from jax.experimental.pallas import tpu as pltpu

ATTN_HEADS = 8
KV_HEADS = 2
HEAD_DIM = 64
Q_PER_KV = ATTN_HEADS // KV_HEADS
ATTN_WIDTH = ATTN_HEADS * HEAD_DIM
KV_WIDTH = KV_HEADS * HEAD_DIM
SSM_GROUP = 16
SSM_STATE = 64
CMP_BLOCK = 32
CMP_STRIDE = 16
SEL_BLOCK = 64
SEL_TOPN = 16
WINDOW = 512
FORCE_SCORE = 1.0e4
ROPE_THETA = 10000.0
CONV_WIDTH = 3
NORM_EPS = 1e-6

LANES = 128
Q_TILE = 128
KEY_CHUNK = 512
ROW_TILE = 512
WIN_CHUNKS = WINDOW // Q_TILE + 1
VMEM_LIMIT = 56 * 1024 * 1024

F32 = jnp.float32
BF16 = jnp.bfloat16
NEG_INF = float("-inf")


def _params(n_axes, vmem=VMEM_LIMIT):
    return pltpu.CompilerParams(dimension_semantics=("arbitrary",) * n_axes,
                                vmem_limit_bytes=vmem)


def _split(a):
    hi = a.astype(BF16)
    lo = (a - hi.astype(F32)).astype(BF16)
    return hi, lo


def _dot(a, b):
    return jnp.dot(a, b, preferred_element_type=F32)


def _dot_hp(a, b):
    a_hi, a_lo = _split(a)
    b_hi, b_lo = _split(b)
    return _dot(a_hi, b_hi) + (_dot(a_hi, b_lo) + _dot(a_lo, b_hi))


def _sigmoid(x):
    return 1.0 / (1.0 + jnp.exp(-x))


def _gelu_tanh(x):
    c = math.sqrt(2.0 / math.pi)
    return 0.5 * x * (1.0 + jnp.tanh(c * (x + 0.044715 * (x * x * x))))


def _rms_scale(x):
    return x * jax.lax.rsqrt(jnp.mean(x * x, axis=-1, keepdims=True) + NORM_EPS)


def _mod_kernel(c_ref, w_ref, b_ref, o_ref):
    c = c_ref[...]
    s = c * _sigmoid(c)
    o_ref[...] = _dot_hp(s, w_ref[...]) + b_ref[...]


def _modulation(c, w_ada, b_ada):
    d, n = w_ada.shape
    bn = n // 4
    c8 = jnp.broadcast_to(c, (8, d))
    out = pl.pallas_call(
        _mod_kernel,
        grid=(n // bn,),
        in_specs=[pl.BlockSpec((8, d), lambda j: (0, 0)),
                  pl.BlockSpec((d, bn), lambda j: (0, j)),
                  pl.BlockSpec((1, bn), lambda j: (0, j))],
        out_specs=pl.BlockSpec((8, bn), lambda j: (0, j)),
        out_shape=jax.ShapeDtypeStruct((8, n), F32),
        compiler_params=_params(1),
        name="mod",
    )(c8, w_ada, b_ada.reshape(1, n))
    return out[0:1]


def _inproj_kernel(x_ref, g_ref, sc_ref, sh_ref, cos_ref, sin_ref, w_ref,
                   qT_ref, kc_ref, ks_ref, kw_ref, vc_ref, vsT_ref, vwT_ref, gT_ref, zs_ref):
    x = x_ref[...]
    h = _rms_scale(x) * g_ref[...] * (1.0 + sc_ref[...]) + sh_ref[...]
    hb = h.astype(BF16)
    cos = cos_ref[...]
    sin = sin_ref[...]
    n_sub = x.shape[0] // Q_TILE
    widths = (ATTN_WIDTH, 3 * KV_WIDTH, 3 * KV_WIDTH, LANES, w_ref.shape[1])
    starts = np.concatenate([[0], np.cumsum(widths[:-1])])
    wq_ref, wk_ref, wv_ref, wg_ref, ws_ref = [
        w_ref.at[:, int(a):int(min(a + n, w_ref.shape[1]))] for a, n in zip(starts, widths)]

    half = HEAD_DIM // 2
    first_half = (jax.lax.broadcasted_iota(jnp.int32, (1, LANES), 1) & (HEAD_DIM - 1)) < half

    def rope(z):
        swapped = jnp.where(first_half, pltpu.roll(z, LANES - half, 1), pltpu.roll(z, half, 1))
        return z * cos + swapped * sin

    zq = _dot(hb, wq_ref[...])
    scale = HEAD_DIM ** -0.5 * math.log2(math.e)
    q_cols = []
    for j in range(ATTN_WIDTH // LANES):
        sl = slice(j * LANES, (j + 1) * LANES)
        q_cols.append(rope(zq[:, sl]) * scale)
    q = jnp.concatenate(q_cols, axis=1)
    for s in range(n_sub):
        qs = q[s * Q_TILE:(s + 1) * Q_TILE, :].T
        qT_ref[s] = qs.reshape(KV_HEADS, Q_PER_KV * HEAD_DIM, Q_TILE)

    zk = _dot(hb, wk_ref[...])
    k_outs = (kc_ref, ks_ref, kw_ref)
    for j in range(3):
        sl = slice(j * LANES, (j + 1) * LANES)
        k_outs[j][...] = rope(zk[:, sl]).astype(k_outs[j].dtype)

    zv = _dot(hb, wv_ref[...])
    vc_ref[...] = zv[:, 0:LANES]
    vs = zv[:, LANES:2 * LANES]
    vw = zv[:, 2 * LANES:3 * LANES]
    for s in range(n_sub):
        vsT_ref[s] = vs[s * Q_TILE:(s + 1) * Q_TILE, :].T.astype(BF16)
        vwT_ref[s] = vw[s * Q_TILE:(s + 1) * Q_TILE, :].T.astype(BF16)

    gates = _sigmoid(_dot(hb, wg_ref[...]))
    for s in range(n_sub):
        gt = gates[s * Q_TILE:(s + 1) * Q_TILE, :].T
        gT_ref[s] = gt[0:2 * 16, :].reshape(KV_HEADS, 16, Q_TILE)

    zs_ref[...] = _dot(hb, ws_ref[...])


def _in_projection(x2, norm_g, sc, sh, w_in):
    l, d = x2.shape
    tm = ROW_TILE
    cuts = np.cumsum([0, ATTN_WIDTH] + [KV_WIDTH] * 6 + [3 * ATTN_HEADS])
    q_cols = np.arange(cuts[0], cuts[1])
    k_cols = np.concatenate([np.arange(cuts[j], cuts[j + 1]) for j in (1, 3, 5)])
    v_cols = np.concatenate([np.arange(cuts[j], cuts[j + 1]) for j in (2, 4, 6)])
    s_cols = np.arange(cuts[8], w_in.shape[1])
    ssm_w = len(s_cols)

    g_cols = np.zeros((LANES,), np.int64)
    g_sign = np.zeros((LANES,), np.int64)
    per_kv = Q_PER_KV * 3
    for hh in range(KV_HEADS):
        g_cols[hh * 16:hh * 16 + per_kv] = cuts[7] + hh * per_kv + np.arange(per_kv)
        g_sign[hh * 16:hh * 16 + per_kv] = 1
    cols = np.concatenate([q_cols, k_cols, v_cols, g_cols, s_cols])
    sign = np.concatenate([np.ones_like(q_cols), np.ones_like(k_cols), np.ones_like(v_cols), g_sign,
                           np.ones_like(s_cols)])
    w_all = (jnp.take(w_in, jnp.asarray(cols, jnp.int32), axis=1)
             * jnp.asarray(sign, F32)[None, :]).astype(BF16)
    weights = [w_all]

    half = HEAD_DIM // 2
    inv = jnp.power(ROPE_THETA, -jnp.arange(half, dtype=F32) * 2.0 / HEAD_DIM)
    ang = jnp.arange(l, dtype=F32)[:, None] * inv[None, :]
    cos = jnp.tile(jnp.cos(ang), (1, LANES // half))
    sin = jnp.tile(jnp.concatenate([-jnp.sin(ang), jnp.sin(ang)], axis=1), (1, LANES // HEAD_DIM))

    n_q = l // Q_TILE
    sub = tm // Q_TILE
    row = lambda i: (i, 0)
    const = lambda i: (0, 0)
    out_shape = [
        jax.ShapeDtypeStruct((n_q, KV_HEADS, Q_PER_KV * HEAD_DIM, Q_TILE), F32),
        jax.ShapeDtypeStruct((l, KV_WIDTH), F32),
        jax.ShapeDtypeStruct((l, KV_WIDTH), BF16),
        jax.ShapeDtypeStruct((l, KV_WIDTH), BF16),
        jax.ShapeDtypeStruct((l, KV_WIDTH), F32),
        jax.ShapeDtypeStruct((n_q, KV_WIDTH, Q_TILE), BF16),
        jax.ShapeDtypeStruct((n_q, KV_WIDTH, Q_TILE), BF16),
        jax.ShapeDtypeStruct((n_q, KV_HEADS, 16, Q_TILE), F32),
        jax.ShapeDtypeStruct((l, ssm_w), F32),
    ]
    out_specs = [
        pl.BlockSpec((sub, KV_HEADS, Q_PER_KV * HEAD_DIM, Q_TILE), lambda i: (i, 0, 0, 0)),
        pl.BlockSpec((tm, KV_WIDTH), row),
        pl.BlockSpec((tm, KV_WIDTH), row),
        pl.BlockSpec((tm, KV_WIDTH), row),
        pl.BlockSpec((tm, KV_WIDTH), row),
        pl.BlockSpec((sub, KV_WIDTH, Q_TILE), lambda i: (i, 0, 0)),
        pl.BlockSpec((sub, KV_WIDTH, Q_TILE), lambda i: (i, 0, 0)),
        pl.BlockSpec((sub, KV_HEADS, 16, Q_TILE), lambda i: (i, 0, 0, 0)),
        pl.BlockSpec((tm, ssm_w), row),
    ]
    in_specs = [pl.BlockSpec((tm, d), row),
                pl.BlockSpec((1, d), const), pl.BlockSpec((1, d), const), pl.BlockSpec((1, d), const),
                pl.BlockSpec((tm, LANES), row), pl.BlockSpec((tm, LANES), row)]
    in_specs += [pl.BlockSpec(w.shape, const) for w in weights]
    return pl.pallas_call(
        _inproj_kernel,
        grid=(l // tm,),
        in_specs=in_specs,
        out_specs=out_specs,
        out_shape=out_shape,
        compiler_params=_params(1),
        name="inproj",
    )(x2, norm_g.reshape(1, d), sc, sh, cos, sin, *weights)


def _compress_kernel(z_ref, pe_ref, w1_ref, w1p_ref, w2_ref, o_ref, *, transpose_out):
    n = z_ref.shape[0] // CMP_STRIDE
    first = jnp.zeros((n, w1_ref.shape[1]), F32)
    second = jnp.zeros((n, w1_ref.shape[1]), F32)
    for j in range(CMP_STRIDE):
        x_j = z_ref[pl.ds(j, n, stride=CMP_STRIDE), :].astype(BF16)
        first = first + _dot(x_j, w1p_ref[0, j])
        second = second + _dot(x_j, w1p_ref[0, CMP_STRIDE + j])
    bias = _dot(pe_ref[...].astype(BF16), w1_ref[...])[0:1]
    h1 = first + pltpu.roll(second, n - 1, 0) + bias
    out = _dot_hp(_gelu_tanh(h1), w2_ref[...])
    if transpose_out:
        o_ref[0] = out.T[0:HEAD_DIM, :]
    else:
        o_ref[0] = out[:, 0:HEAD_DIM]


def _compress(z, pe, w1, w2, transpose_out):
    l = z.shape[0]
    n = l // CMP_STRIDE
    hidden = w1.shape[1]
    assert CMP_BLOCK == 2 * CMP_STRIDE
    pe8 = jnp.broadcast_to(pe.reshape(1, CMP_BLOCK * HEAD_DIM), (8, CMP_BLOCK * HEAD_DIM))
    w1b = w1.astype(BF16)
    w1r = w1b.reshape(CMP_BLOCK, 1, HEAD_DIM, hidden)
    head_sel = jnp.eye(KV_HEADS, dtype=BF16)[:, None, :, None, None]
    w1p = (head_sel * w1r[None]).reshape(KV_HEADS, CMP_BLOCK, KV_WIDTH, hidden)
    w2p = jnp.pad(w2, ((0, 0), (0, LANES - HEAD_DIM)))
    oshape = (KV_HEADS, HEAD_DIM, n) if transpose_out else (KV_HEADS, n, HEAD_DIM)
    return pl.pallas_call(
        functools.partial(_compress_kernel, transpose_out=transpose_out),
        grid=(KV_HEADS,),
        in_specs=[pl.BlockSpec(z.shape, lambda h: (0, 0)),
                  pl.BlockSpec(pe8.shape, lambda h: (0, 0)),
                  pl.BlockSpec(w1b.shape, lambda h: (0, 0)),
                  pl.BlockSpec((1,) + w1p.shape[1:], lambda h: (h, 0, 0, 0)),
                  pl.BlockSpec(w2p.shape, lambda h: (0, 0))],
        out_specs=pl.BlockSpec((1,) + oshape[1:], lambda h: (h, 0, 0)),
        out_shape=jax.ShapeDtypeStruct(oshape, F32),
        compiler_params=_params(1),
        name="compress_v" if transpose_out else "compress_k",
    )(z, pe8, w1b, w1p, w2p)


def _tile4(a):
    return jnp.concatenate([a] * Q_PER_KV, axis=1)


MASKED = -1e30
P_PAD = 8


LOOP_STEPS = 2
SEL_STEP = 32
N_FORCED = 3


def _attn_kernel(qT_ref, kc_ref, vcT_ref, ks_ref, vsT_ref, *rest):
    kw_refs = rest[0:WIN_CHUNKS]
    vw_refs = rest[WIN_CHUNKS:2 * WIN_CHUNKS]
    gT_ref, o_ref, p_scr, bias_scr, own_scr, oc_scr, ow_scr, stage_scr = rest[2 * WIN_CHUNKS:2 * WIN_CHUNKS + 8]
    per_head = rest[2 * WIN_CHUNKS + 8:]
    H, G, DH, TQ = KV_HEADS, Q_PER_KV, HEAD_DIM, Q_TILE
    s_bufs = [per_head[k * H:(k + 1) * H] for k in range(LOOP_STEPS)]
    p_bufs = [per_head[(LOOP_STEPS + k) * H:(LOOP_STEPS + k + 1) * H] for k in range(LOOP_STEPS)]
    i = pl.program_id(0)
    n_sel = bias_scr.shape[0]
    W = G * TQ
    heads = range(H)

    q_all, q_pad = [], []
    for h in heads:
        qT = qT_ref[0, h]
        qa = jnp.concatenate([qT[g * DH:(g + 1) * DH, :] for g in range(G)], axis=1)
        zeros = jnp.zeros_like(qa)
        q_all.append(qa)
        q_pad.append(jnp.concatenate([qa if hh == h else zeros for hh in heads], axis=0).astype(BF16))
    v_rows = [slice(h * DH, (h + 1) * DH) for h in heads]

    t = i * TQ + jax.lax.broadcasted_iota(jnp.int32, (1, TQ), 1)
    cur = jax.lax.shift_right_logical(t, int(math.log2(SEL_BLOCK)))

    ratio = SEL_BLOCK // CMP_STRIDE

    def select(ns):
        nc = ns * ratio
        c_end = jax.lax.broadcasted_iota(jnp.int32, (nc, 1), 0) * CMP_STRIDE + (CMP_BLOCK - 1)
        c_bias = _tile4(jnp.where(c_end <= t, 0.0, MASKED))
        blk = jax.lax.broadcasted_iota(jnp.int32, (ns, 1), 0)
        blk_f = blk.astype(F32)
        forced = (blk == 0) | (blk == cur) | (blk == cur - 1)
        scores = []
        for h in heads:
            stage_scr[h, 0:nc, :] = _dot(kc_ref[h, 0:nc, :].astype(BF16), q_all[h].astype(BF16))
        for h in heads:
            s = stage_scr[h, 0:nc, :] + c_bias
            m = jnp.max(s, axis=0, keepdims=True)
            e = jnp.exp2(s - m)
            any_valid = jnp.where(m > 0.5 * MASKED, 1.0, 0.0)
            p = e * (any_valid / jnp.sum(e, axis=0, keepdims=True))
            oc_scr[h] = _dot(vcT_ref[h, :, 0:nc].astype(BF16), p.astype(BF16))
            p_sum = p[:, 0:TQ]
            for g in range(1, G):
                p_sum = p_sum + p[:, g * TQ:(g + 1) * TQ]
            p_scr[h, 0:P_PAD, :] = jnp.zeros((P_PAD, TQ), F32)
            p_scr[h, P_PAD:P_PAD + nc, :] = p_sum

            def every(off):
                return p_scr[h, pl.ds(P_PAD + off, ns, stride=ratio), :]

            imp = 0.5 * (every(-1) + every(3)) + (every(0) + every(1) + every(2))
            scores.append(jnp.where((blk > cur) | forced, NEG_INF, imp))
        score = jnp.concatenate(scores, axis=1)

        def pick_one(_, score):
            m = jnp.max(score, axis=0, keepdims=True)
            first = jnp.min(jnp.where(score == m, blk_f, float(ns)), axis=0, keepdims=True)
            return jnp.where(blk_f == first, NEG_INF, score)

        left = jax.lax.fori_loop(0, SEL_TOPN - N_FORCED, pick_one, score)
        cur_w = jnp.concatenate([cur] * H, axis=1)
        bias = jnp.where((left == NEG_INF) & (blk <= cur_w), 0.0, MASKED)
        own_scr[0:1, :] = jnp.max(jnp.where(blk == 2 * i, bias, MASKED), axis=0, keepdims=True)
        own_scr[1:2, :] = jnp.max(jnp.where(blk == 2 * i + 1, bias, MASKED), axis=0, keepdims=True)
        bias_scr[0:ns, :] = jnp.where((blk == 2 * i) | (blk == 2 * i + 1), MASKED, bias)
        if ns < n_sel:
            bias_scr[ns:n_sel, :] = jnp.full((n_sel - ns, H * TQ), MASKED, F32)

    sizes = [min(n_sel, SEL_STEP * (k + 1)) for k in range(pl.cdiv(n_sel, SEL_STEP))]
    variant = jnp.minimum((2 * i + 1) // SEL_STEP, len(sizes) - 1)
    for k, ns in enumerate(sizes):
        pl.when(variant == k)(functools.partial(select, ns))

    KC = KEY_CHUNK
    BPC = KC // SEL_BLOCK
    VROWS = DH + 16
    onehot = (jax.lax.broadcasted_iota(jnp.int32, (KC, LANES), 1)
              == jax.lax.shift_right_logical(jax.lax.broadcasted_iota(jnp.int32, (KC, LANES), 0),
                                             int(math.log2(SEL_BLOCK)))).astype(BF16)
    ones_rows = jnp.ones((VROWS - DH, KC), BF16)
    pad_rows = jnp.zeros((LANES - BPC, W), F32)
    last_chunk = ks_ref.shape[0] // KC - 1

    n_main = (i + KC // TQ - 1) // (KC // TQ)

    def augmented_q(c):
        rows = bias_scr[pl.ds(pl.multiple_of(jnp.minimum(c, last_chunk) * BPC, BPC), BPC), :]
        rows = jnp.where(c < n_main, rows, MASKED)
        out = []
        for h in heads:
            extra = jnp.concatenate([_tile4(rows[:, h * TQ:(h + 1) * TQ]), pad_rows], axis=0)
            out.append(jnp.concatenate([q_pad[h], extra.astype(BF16)], axis=0))
        return out

    def scores(c):
        k = ks_ref[pl.ds(pl.multiple_of(jnp.minimum(c, last_chunk) * KC, KC), KC), :]
        k_aug = jnp.concatenate([k, onehot], axis=1)
        q_aug = augmented_q(c)
        out = []
        for h in heads:
            s = _dot(k_aug, q_aug[h])
            out.append((s, jnp.max(s, axis=0, keepdims=True)))
        return out

    def values(c):
        c = jnp.clip(c, 0, last_chunk)
        v = jnp.concatenate([vsT_ref[(KC // TQ) * c + j] for j in range(KC // TQ)], axis=1)
        return [jnp.concatenate([v[v_rows[h], :], ones_rows], axis=0) for h in heads]

    def chunk_step(c, m_run, acc, mx_cur, s_cur, s_next, p_prev, p_cur):
        v_prev = values(c - 1)
        nxt = scores(c + 1)
        m_new, mx_next = [], []
        for h in heads:
            s_next[h][...] = nxt[h][0]
            mx_next.append(nxt[h][1])
        for h in heads:
            acc[h] = acc[h] + _dot(v_prev[h], p_prev[h][...])
        for h in heads:
            m_new.append(jnp.maximum(m_run[h], mx_cur[h]))
            acc[h] = jnp.exp2(m_run[h] - m_new[h]) * acc[h]
            p_cur[h][...] = jnp.exp2(s_cur[h][...] - m_new[h]).astype(BF16)
        return m_new, acc, mx_next

    B = LOOP_STEPS

    def body(j, carry):
        m_run, acc, mx = (list(c) for c in carry)
        for k in range(B):
            m_run, acc, mx = chunk_step(B * j + k, m_run, acc, mx,
                                        s_bufs[k], s_bufs[(k + 1) % B], p_bufs[(k - 1) % B], p_bufs[k])
        return tuple(m_run), tuple(acc), tuple(mx)

    n_iter = (n_main + B - 1) // B

    k_own = ks_ref[pl.ds(pl.multiple_of(i * TQ, TQ), TQ), :]
    v_own = vsT_ref[i]
    k_w = jnp.concatenate([r[...] for r in kw_refs], axis=0)
    v_w = jnp.concatenate([r[0] for r in vw_refs], axis=1)
    n_win = WIN_CHUNKS * TQ
    w_pos = (i - (WIN_CHUNKS - 1)) * TQ + jax.lax.broadcasted_iota(jnp.int32, (n_win, 1), 0)
    w_bias = _tile4(jnp.where((w_pos <= t) & (w_pos > t - WINDOW) & (w_pos >= 0), 0.0, MASKED))
    ones_w = jnp.ones((VROWS - DH, n_win), BF16)
    for h in heads:
        stage_scr[h, 0:TQ, :] = _dot(k_own, q_pad[h])
        stage_scr[h, TQ:TQ + n_win, :] = _dot(k_w, q_pad[h])
    s_0 = scores(0)
    for h in heads:
        s_bufs[0][h][...] = s_0[h][0]
        p_bufs[B - 1][h][...] = jnp.zeros(p_bufs[B - 1][h].shape, BF16)
    for h in heads:
        s = stage_scr[h, TQ:TQ + n_win, :] + w_bias
        p = jnp.exp2(s - jnp.max(s, axis=0, keepdims=True)).astype(BF16)
        a_w = _dot(jnp.concatenate([v_w[v_rows[h], :], ones_w], axis=0), p)
        ow_scr[h] = a_w[0:DH, :] / a_w[DH:DH + 1, :]

    init = (tuple(jnp.full((1, W), MASKED, F32) for _ in heads),
            tuple(jnp.zeros((VROWS, W), F32) for _ in heads),
            tuple(s_0[h][1] for h in heads))
    m_run, acc, _ = jax.lax.fori_loop(0, n_iter, body, init)
    v_last = values(B * n_iter - 1)
    acc = tuple(acc[h] + _dot(v_last[h], p_bufs[B - 1][h][...]) for h in heads)

    row_o = jax.lax.broadcasted_iota(jnp.int32, (TQ, 1), 0)
    lane_o = jax.lax.broadcasted_iota(jnp.int32, (1, TQ), 1)
    head_out = []
    for h in heads:
        a = acc[h]
        own_lo = own_scr[0:1, h * TQ:(h + 1) * TQ]
        own_hi = own_scr[1:2, h * TQ:(h + 1) * TQ]
        own_bias = jnp.where(row_o <= lane_o, jnp.where(row_o < SEL_BLOCK, own_lo, own_hi), MASKED)
        s = stage_scr[h, 0:TQ, :] + _tile4(own_bias)
        m_s = jnp.maximum(m_run[h], jnp.max(s, axis=0, keepdims=True))
        p = jnp.exp2(s - m_s).astype(BF16)
        a = jnp.exp2(m_run[h] - m_s) * a + _dot(jnp.concatenate([v_own[v_rows[h], :], ones_rows[:, 0:TQ]], axis=0), p)
        o_s = a[0:DH, :] / a[DH:DH + 1, :]

        o_w = ow_scr[h]
        o_c = oc_scr[h]
        gates = gT_ref[0, h]
        for g in range(G):
            cols = slice(g * TQ, (g + 1) * TQ)
            head_out.append(gates[3 * g:3 * g + 1, :] * o_c[:, cols]
                            + gates[3 * g + 1:3 * g + 2, :] * o_s[:, cols]
                            + gates[3 * g + 2:3 * g + 3, :] * o_w[:, cols])
    o_ref[...] = jnp.concatenate(head_out, axis=0).T


def _attention(qT, kc, vcT, ks, vsT, kw, vwT, gT, l):
    n_q = l // Q_TILE
    n_cmp = l // CMP_STRIDE
    n_sel = l // SEL_BLOCK
    assert CMP_BLOCK == 2 * CMP_STRIDE and SEL_BLOCK == 4 * CMP_STRIDE

    assert n_sel >= SEL_TOPN and KV_WIDTH == LANES
    H, W = KV_HEADS, Q_PER_KV * Q_TILE

    def win(j):
        return lambda i: (jnp.maximum(i - (WIN_CHUNKS - 1) + j, 0), 0)

    def win3(j):
        return lambda i: (jnp.maximum(i - (WIN_CHUNKS - 1) + j, 0), 0, 0)

    in_specs = [
        pl.BlockSpec((1, H, Q_PER_KV * HEAD_DIM, Q_TILE), lambda i: (i, 0, 0, 0)),
        pl.BlockSpec(kc.shape, lambda i: (0, 0, 0)),
        pl.BlockSpec(vcT.shape, lambda i: (0, 0, 0)),
        pl.BlockSpec(ks.shape, lambda i: (0, 0)),
        pl.BlockSpec(vsT.shape, lambda i: (0, 0, 0)),
    ]
    in_specs += [pl.BlockSpec((Q_TILE, KV_WIDTH), win(j)) for j in range(WIN_CHUNKS)]
    in_specs += [pl.BlockSpec((1, KV_WIDTH, Q_TILE), win3(j)) for j in range(WIN_CHUNKS)]
    in_specs += [pl.BlockSpec((1, H, 16, Q_TILE), lambda i: (i, 0, 0, 0))]
    return pl.pallas_call(
        _attn_kernel,
        grid=(n_q,),
        in_specs=in_specs,
        out_specs=pl.BlockSpec((Q_TILE, ATTN_WIDTH), lambda i: (i, 0)),
        out_shape=jax.ShapeDtypeStruct((l, ATTN_WIDTH), F32),
        scratch_shapes=[pltpu.VMEM((H, P_PAD + n_cmp, Q_TILE), F32),
                        pltpu.VMEM((n_sel, H * Q_TILE), F32),
                        pltpu.VMEM((8, H * Q_TILE), F32),
                        pltpu.VMEM((H, HEAD_DIM, W), F32),
                        pltpu.VMEM((H, HEAD_DIM, W), F32),
                        pltpu.VMEM((H, max(n_cmp, (WIN_CHUNKS + 1) * Q_TILE), W), F32)]
        + [pltpu.VMEM((KEY_CHUNK, W), F32)] * (LOOP_STEPS * H)
        + [pltpu.VMEM((KEY_CHUNK, W), BF16)] * (LOOP_STEPS * H),
        compiler_params=_params(1),
        name="attn",
    )(qT, kc, vcT, ks, vsT, *([kw] * WIN_CHUNKS), *([vwT] * WIN_CHUNKS), gT)


SSM_TILE = 256
SSM_PACK = 16


def _ssm_kernel(u_ref, a_ref, wb_ref, wc_ref, msk_ref, d_ref, wg_ref, bg_ref, y_ref, x_scr, st_scr):
    step = pl.program_id(0)
    tc, width = u_ref.shape
    P = SSM_PACK

    @pl.when(step == 0)
    def _():
        st_scr[...] = jnp.zeros_like(st_scr)

    PH = P // 2
    hw = width // 2
    msk = msk_ref[...]
    u = u_ref[...]

    for hf in range(2):
        u_half = u[:, hf * hw:(hf + 1) * hw]
        wb_half = wb_ref[hf * hw:(hf + 1) * hw, :]
        for j in range(PH):
            bu = _dot((u_half * msk[j:j + 1, :]).astype(BF16), wb_half)
            x_scr[hf, 0, pl.ds(j, tc, stride=PH), :] = bu[:, 0:LANES]
            x_scr[hf, 1, pl.ds(j, tc, stride=PH), :] = bu[:, LANES:2 * LANES]

    a_re = [a_ref[0, hf * PH:(hf + 1) * PH, :] for hf in range(2)]
    a_im = [a_ref[1, hf * PH:(hf + 1) * PH, :] for hf in range(2)]

    def one(ti, carry):
        rows = pl.ds(pl.multiple_of(ti * PH, PH), PH)
        out = []
        for hf in range(2):
            x_re, x_im = carry[hf]
            n_re = a_re[hf] * x_re - a_im[hf] * x_im + x_scr[hf, 0, rows, :]
            n_im = a_re[hf] * x_im + a_im[hf] * x_re + x_scr[hf, 1, rows, :]
            x_scr[hf, 0, rows, :] = n_re
            x_scr[hf, 1, rows, :] = n_im
            out.append((n_re, n_im))
        return tuple(out)

    init = tuple((st_scr[0, hf * PH:(hf + 1) * PH, :], st_scr[1, hf * PH:(hf + 1) * PH, :]) for hf in range(2))
    last = jax.lax.fori_loop(0, tc, one, init, unroll=8)
    for hf in range(2):
        st_scr[0, hf * PH:(hf + 1) * PH, :] = last[hf][0]
        st_scr[1, hf * PH:(hf + 1) * PH, :] = last[hf][1]

    halves = []
    for hf in range(2):
        wc_half = wc_ref[:, hf * hw:(hf + 1) * hw]
        y_half = jnp.zeros((tc, hw), F32)
        for j in range(PH):
            x_j = jnp.concatenate([x_scr[hf, 0, pl.ds(j, tc, stride=PH), :],
                                   x_scr[hf, 1, pl.ds(j, tc, stride=PH), :]], axis=1).astype(BF16)
            y_half = y_half + msk[j:j + 1, :] * _dot(x_j, wc_half)
        halves.append(y_half)
    y = _gelu_tanh(jnp.concatenate(halves, axis=1) + d_ref[...] * u)
    y_ref[...] = y * _sigmoid(_dot(y.astype(BF16), wg_ref[...]) + bg_ref[...])


def _s5(zs, lam_re, lam_im, log_step, b_re, b_im, c_re, c_im, d_skip, w_glu, b_glu):
    l, width = zs.shape
    G, Pn, C = b_re.shape
    assert Pn == SSM_STATE and C == SSM_GROUP and G * C == width and G == 2 * SSM_PACK
    step = jnp.exp(log_step)[:, None]
    mag = jnp.exp(lam_re * step)
    a_re = mag * jnp.cos(lam_im * step)
    a_im = mag * jnp.sin(lam_im * step)
    den = lam_re * lam_re + lam_im * lam_im
    n_re = a_re - 1.0
    f_re = (n_re * lam_re + a_im * lam_im) / den
    f_im = (a_im * lam_re - n_re * lam_im) / den
    bb_re = f_re[..., None] * b_re - f_im[..., None] * b_im
    bb_im = f_re[..., None] * b_im + f_im[..., None] * b_re
    a_pack = jnp.stack([a_re.reshape(SSM_PACK, 2 * Pn), a_im.reshape(SSM_PACK, 2 * Pn)])

    eye2 = jnp.eye(2, dtype=F32)

    def b_stack(bb):
        t = bb.reshape(SSM_PACK, 2, Pn, C)
        return jnp.einsum('jgpc,gh->jhcgp', t, eye2).reshape(width, 2 * Pn)

    w_b = jnp.concatenate([b_stack(bb_re), b_stack(bb_im)], axis=1).astype(BF16)

    def c_stack(cc):
        t = cc.reshape(SSM_PACK, 2, C, Pn)
        return jnp.einsum('jgcp,gh->gpjhc', t, eye2).reshape(2 * Pn, width)

    w_c = jnp.concatenate([c_stack(c_re), -c_stack(c_im)], axis=0).astype(BF16)
    half_rows, half_w = SSM_PACK // 2, width // 2
    lane_row = jnp.arange(half_w) // (2 * C)
    msk = (lane_row[None, :] == jnp.arange(half_rows)[:, None]).astype(F32)

    tc = SSM_TILE
    const2 = lambda i: (0, 0)
    return pl.pallas_call(
        _ssm_kernel,
        grid=(l // tc,),
        in_specs=[pl.BlockSpec((tc, width), lambda i: (i, 0)),
                  pl.BlockSpec(a_pack.shape, lambda i: (0, 0, 0)),
                  pl.BlockSpec(w_b.shape, const2),
                  pl.BlockSpec(w_c.shape, const2),
                  pl.BlockSpec(msk.shape, const2),
                  pl.BlockSpec((1, width), const2),
                  pl.BlockSpec((width, width), const2),
                  pl.BlockSpec((1, width), const2)],
        out_specs=pl.BlockSpec((tc, width), lambda i: (i, 0)),
        out_shape=jax.ShapeDtypeStruct((l, width), F32),
        scratch_shapes=[pltpu.VMEM((2, 2, tc * half_rows, LANES), F32),
                        pltpu.VMEM((2, SSM_PACK, LANES), F32)],
        compiler_params=_params(1),
        name="ssm",
    )(zs, a_pack, w_b, w_c, msk, d_skip.reshape(1, width), w_glu.astype(BF16), b_glu.reshape(1, width))


def _outproj_kernel(x_ref, oa_ref, ys_ref, ga_ref, gs_ref, wa_ref, ws_ref, gt_ref, o_ref):
    a = (_rms_scale(oa_ref[...]) * ga_ref[...]).astype(BF16)
    s = (_rms_scale(ys_ref[...]) * gs_ref[...]).astype(BF16)
    y = _dot(a, wa_ref[...]) + _dot(s, ws_ref[...])
    o_ref[...] = x_ref[...] + gt_ref[...] * y


def _out_projection(x2, o_attn, y_ssm, attn_g, ssm_g, w_o, gt):
    l, d = x2.shape
    tm = ROW_TILE
    aw, sw = o_attn.shape[1], y_ssm.shape[1]
    row = lambda i: (i, 0)
    const = lambda i: (0, 0)
    return pl.pallas_call(
        _outproj_kernel,
        grid=(l // tm,),
        in_specs=[pl.BlockSpec((tm, d), row), pl.BlockSpec((tm, aw), row), pl.BlockSpec((tm, sw), row),
                  pl.BlockSpec((1, aw), const), pl.BlockSpec((1, sw), const),
                  pl.BlockSpec((aw, d), const), pl.BlockSpec((sw, d), const), pl.BlockSpec((1, d), const)],
        out_specs=pl.BlockSpec((tm, d), row),
        out_shape=jax.ShapeDtypeStruct((l, d), F32),
        compiler_params=_params(1),
        name="outproj",
    )(x2, o_attn, y_ssm, attn_g.reshape(1, aw), ssm_g.reshape(1, sw),
      w_o[:aw].astype(BF16), w_o[aw:].astype(BF16), gt)


FFN_TILE = 256
FFN_COLS = 256
HALO = 8


def _ffn_kernel(x_ref, g_ref, sc_ref, sh_ref, wup_ref, cw_ref, cb_ref, wdn_ref, gt_ref, gf_ref,
                o_ref, ua_scr, uv_scr):
    step = pl.program_id(0)
    tm = x_ref.shape[0]
    n_col = wdn_ref.shape[0] // FFN_COLS

    @pl.when(step == 0)
    def _():
        ua_scr[...] = jnp.zeros_like(ua_scr)
        uv_scr[...] = jnp.zeros_like(uv_scr)

    x = x_ref[...]
    hb = (_rms_scale(x) * g_ref[...] * (1.0 + sc_ref[...]) + sh_ref[...]).astype(BF16)

    cw = FFN_COLS

    def up(scr, j, col):
        scr[j, 0:HALO, :] = scr[j, tm:tm + HALO, :]
        scr[j, HALO:tm + HALO, :] = _dot(hb, wup_ref[:, col * cw:(col + 1) * cw])

    def conv(scr, j, col):
        cols = slice(col * cw, (col + 1) * cw)
        w = cw_ref[:, cols]
        out = cb_ref[:, cols]
        for k in range(CONV_WIDTH):
            off = HALO - (CONV_WIDTH - 1) + k
            out = out + w[k:k + 1, :] * scr[j, off:off + tm, :]
        return out

    y = jnp.zeros(x.shape, F32)
    for j in range(n_col):
        up(ua_scr, j, j)
        up(uv_scr, j, n_col + j)
    for j in range(n_col):
        a = conv(ua_scr, j, j)
        v = conv(uv_scr, j, n_col + j)
        act = (a * _sigmoid(a) * v).astype(BF16)
        y = y + _dot(act, wdn_ref[j * cw:(j + 1) * cw, :])
    x2 = x + gt_ref[...] * y
    o_ref[...] = _rms_scale(x2) * gf_ref[...]


def _conv_ffn(x1, norm_g, sc, sh, w_up, conv_w, conv_b, w_down, gt, normf_g):
    l, d = x1.shape
    f = w_down.shape[0]
    tm, cw = FFN_TILE, FFN_COLS
    n_col = f // cw
    assert f % cw == 0
    wup = w_up.astype(BF16)
    cbt = conv_b.reshape(1, 2 * f)
    wdn = w_down.astype(BF16)
    row = lambda i: (i, 0)
    const = lambda i: (0, 0)
    return pl.pallas_call(
        _ffn_kernel,
        grid=(l // tm,),
        in_specs=[pl.BlockSpec((tm, d), row),
                  pl.BlockSpec((1, d), const), pl.BlockSpec((1, d), const), pl.BlockSpec((1, d), const),
                  pl.BlockSpec(wup.shape, const), pl.BlockSpec(conv_w.shape, const),
                  pl.BlockSpec(cbt.shape, const), pl.BlockSpec(wdn.shape, const),
                  pl.BlockSpec((1, d), const), pl.BlockSpec((1, d), const)],
        out_specs=pl.BlockSpec((tm, d), row),
        out_shape=jax.ShapeDtypeStruct((l, d), F32),
        scratch_shapes=[pltpu.VMEM((n_col, tm + HALO, cw), F32),
                        pltpu.VMEM((n_col, tm + HALO, cw), F32)],
        compiler_params=_params(1),
        name="ffn",
    )(x1, norm_g.reshape(1, d), sc, sh, wup, conv_w, cbt, wdn, gt, normf_g.reshape(1, d))


def kernel(x, c, norm1_g, norm2_g, normf_g, w_ada, b_ada, w_in, pe_k, pe_v, w_ck1, w_ck2, w_cv1, w_cv2,
           lam_re, lam_im, log_step, b_re, b_im, c_re, c_im, d_skip, w_glu, b_glu, attn_norm_g,
           ssm_norm_g, w_o, w_up, conv_w, conv_b, w_down):
    b, l, d = x.shape
    depth = w_ada.shape[0]
    assert b == 1 and depth == 1
    assert l % KEY_CHUNK == 0 and l % ROW_TILE == 0 and l >= WINDOW
    x2 = x.reshape(l, d)
    mod = _modulation(c, w_ada[0], b_ada[0])
    sh1, sc1, gt1, sh2, sc2, gt2 = [mod[:, k * d:(k + 1) * d] for k in range(6)]

    qT, k_c, k_s, k_w, v_c, v_sT, v_wT, gT, zs = _in_projection(x2, norm1_g[0], sc1, sh1, w_in[0])
    kc = _compress(k_c, pe_k[0], w_ck1[0], w_ck2[0], transpose_out=False)
    vcT = _compress(v_c, pe_v[0], w_cv1[0], w_cv2[0], transpose_out=True)
    o_attn = _attention(qT, kc, vcT, k_s, v_sT, k_w, v_wT, gT, l)
    y_ssm = _s5(zs, lam_re[0], lam_im[0], log_step[0], b_re[0], b_im[0], c_re[0], c_im[0],
                d_skip[0], w_glu[0], b_glu[0])
    x1 = _out_projection(x2, o_attn, y_ssm, attn_norm_g[0], ssm_norm_g[0], w_o[0], gt1)
    out = _conv_ffn(x1, norm2_g[0], sc2, sh2, w_up[0], conv_w[0], conv_b[0], w_down[0], gt2, normf_g)
    return out.reshape(b, l, d)
```
